```python
import jax, jax.numpy as jnp
from jax import lax
import numpy as np

D_MODEL = 1024
BATCH = 1
SEQ = 16384
DEPTH = 2
DEC_BATCH = 32
DEC_SEQ = 8
PAST_LEN = 16384
PAGE_SIZE = 128

CONV_DIM = 1024
CONV_WIDTH = 3
N_HEADS = 8
N_KV_HEADS = 2
HEAD_DIM = 128
ATT_DIM = N_HEADS * HEAD_DIM
N_IDX_HEADS = 8
IDX_DIM = 64
TOPK_MAX = 256
Q_BLOCK = 128
EPS = 1e-6
NEG_INF = -1e30

kernel_name = "gated_conv_dsa_parallel_hybrid_step"


def _proj_sizes():
    return (CONV_DIM, CONV_DIM, CONV_DIM, CONV_DIM,
            ATT_DIM, N_KV_HEADS * HEAD_DIM, N_KV_HEADS * HEAD_DIM,
            N_IDX_HEADS * IDX_DIM, IDX_DIM, N_IDX_HEADS, ATT_DIM,
            D_MODEL, D_MODEL)


def _split_proj(p):
    offs = np.cumsum(np.array(_proj_sizes()))[:-1].tolist()
    return jnp.split(p, offs, axis=-1)


def _rms(x, g):
    xf = x.astype(jnp.float32)
    xf = xf * lax.rsqrt(jnp.mean(xf * xf, axis=-1, keepdims=True) + EPS)
    return xf.astype(x.dtype) * g


def _adaln(x, c, norm_g, w_mod, b_mod):
    mod = jax.nn.silu(c) @ w_mod + b_mod
    shift, scale, gate = jnp.split(mod, 3, axis=-1)
    h = _rms(x, norm_g) * (1 + scale[:, None, :]) + shift[:, None, :]
    return h, gate[:, None, :]


def _short_conv(prev, u, conv_w):
    t = u.shape[1]
    buf = jnp.concatenate([prev.astype(u.dtype), u], axis=1)
    y = conv_w[0] * buf[:, 0:t]
    for j in range(1, CONV_WIDTH):
        y = y + conv_w[j] * buf[:, j:j + t]
    return y, buf[:, -(CONV_WIDTH - 1):]


def _index_scores(qi, wi, kidx):
    s = jnp.einsum('btid,bsd->btis', qi.astype(jnp.float32), kidx.astype(jnp.float32)) * (IDX_DIM ** -0.5)
    w = wi.astype(jnp.float32) * (N_IDX_HEADS ** -0.5)
    return jnp.einsum('btis,bti->bts', jax.nn.relu(s), w)


def _sparse_attend(q, k_sel, v_sel, valid):
    b, t = q.shape[:2]
    qg = q.reshape(b, t, N_KV_HEADS, N_HEADS // N_KV_HEADS, HEAD_DIM).astype(jnp.float32)
    logits = jnp.einsum('btkgd,btnkd->btkgn', qg, k_sel.astype(jnp.float32)) * (HEAD_DIM ** -0.5)
    logits = jnp.where(valid[:, :, None, None, :], logits, NEG_INF)
    p = jax.nn.softmax(logits, axis=-1)
    o = jnp.einsum('btkgn,btnkd->btkgd', p, v_sel.astype(jnp.float32))
    return o.reshape(b, t, ATT_DIM).astype(q.dtype)


def _gather_rows(a, idx):
    return jax.vmap(lambda ab, ib: ab[ib])(a, idx)


def _prompt_attention(q, qi, wi, k, v, kidx):
    b, s = q.shape[:2]
    topk = min(TOPK_MAX, s // 4)
    nb = s // Q_BLOCK
    key_pos = jnp.arange(s)

    def blk(args):
        qb, qib, wib, pos = args
        sc = _index_scores(qib, wib, kidx)
        sc = jnp.where(key_pos[None, None, :] <= pos[None, :, None], sc, NEG_INF)
        _, idx = lax.top_k(sc, topk)
        valid = idx <= pos[None, :, None]
        return _sparse_attend(qb, _gather_rows(k, idx), _gather_rows(v, idx), valid)

    def to_blocks(a):
        return jnp.moveaxis(a.reshape((b, nb, Q_BLOCK) + a.shape[2:]), 1, 0)

    out = lax.map(blk, (to_blocks(q), to_blocks(qi), to_blocks(wi),
                        jnp.arange(s).reshape(nb, Q_BLOCK)))
    return jnp.moveaxis(out, 0, 1).reshape(b, s, ATT_DIM)


def _sample_attention(q, qi, wi, k_new, v_new, kidx_new, pool_k, pool_v, pool_kidx, page_table):
    db, t = q.shape[:2]
    n_pages = page_table.shape[1]
    page = pool_k.shape[1]
    past = n_pages * page
    total = past + t
    topk = min(TOPK_MAX, total // 4)
    kidx_past = pool_kidx[page_table].reshape(db, past, IDX_DIM)
    kidx_all = jnp.concatenate([kidx_past.astype(kidx_new.dtype), kidx_new], axis=1)
    qpos = past + jnp.arange(t)
    sc = _index_scores(qi, wi, kidx_all)
    sc = jnp.where(jnp.arange(total)[None, None, :] <= qpos[None, :, None], sc, NEG_INF)
    _, idx = lax.top_k(sc, topk)
    valid = idx <= qpos[None, :, None]
    in_past = idx < past
    pidx = jnp.minimum(idx, past - 1)
    phys = jax.vmap(lambda pt, i: pt[i])(page_table, pidx // page)
    off = pidx % page
    nidx = jnp.clip(idx - past, 0, t - 1)

    def pick(pool, new):
        old = pool[phys, off]
        cur = _gather_rows(new, nidx)
        return jnp.where(in_past[..., None, None], old.astype(new.dtype), cur)

    return _sparse_attend(q, pick(pool_k, k_new), pick(pool_v, v_new), valid)


def _layer(x, c, conv_prev, attend, norm_g, w_mod, b_mod, w_in, conv_w, q_norm, k_norm, w_pa, w_pb, w_out):
    b, t, _ = x.shape
    h, gate = _adaln(x, c, norm_g, w_mod, b_mod)
    (u_h, u_b, u_c, z_a, q, k, v, qi, ki, wi, z_b, g_a, g_b) = _split_proj(h @ w_in)
    q = _rms(q.reshape(b, t, N_HEADS, HEAD_DIM), q_norm)
    k = _rms(k.reshape(b, t, N_KV_HEADS, HEAD_DIM), k_norm)
    v = v.reshape(b, t, N_KV_HEADS, HEAD_DIM)
    qi = qi.reshape(b, t, N_IDX_HEADS, IDX_DIM)
    conv_out, conv_state = _short_conv(conv_prev, u_c * u_h, conv_w)
    branch_a = (u_b * conv_out) * jax.nn.silu(z_a)
    branch_b = attend(q, qi, wi, k, v, ki) * jax.nn.silu(z_b)
    merged = jax.nn.sigmoid(g_a) * (branch_a @ w_pa) + jax.nn.sigmoid(g_b) * (branch_b @ w_pb)
    y = x + gate * (merged @ w_out)
    return y, k, v, ki, conv_state


def setup_inputs(seed: int = 0) -> dict:
    key = jax.random.key(seed)
    ks = jax.random.split(key, 24)
    f32 = jnp.float32
    n_pages = PAST_LEN // PAGE_SIZE
    n_used = DEC_BATCH * n_pages
    n_phys = n_used + max(1, n_used // 4)
    p_cols = int(sum(_proj_sizes()))
    page_table = jax.random.permutation(ks[0], n_phys)[:n_used].reshape(DEC_BATCH, n_pages).astype(jnp.int32)
    nrm = lambda k, shape, s: jax.random.normal(k, shape, f32) * s
    return {
        "x_prompt": nrm(ks[1], (BATCH, SEQ, D_MODEL), 1.0),
        "x_sample": nrm(ks[2], (DEC_BATCH, DEC_SEQ, D_MODEL), 1.0),
        "c_prompt": nrm(ks[3], (BATCH, D_MODEL), 1.0),
        "c_sample": nrm(ks[4], (DEC_BATCH, D_MODEL), 1.0),
        "cache_k": nrm(ks[5], (DEPTH, n_phys, PAGE_SIZE, N_KV_HEADS, HEAD_DIM), 1.0),
        "cache_v": nrm(ks[6], (DEPTH, n_phys, PAGE_SIZE, N_KV_HEADS, HEAD_DIM), 1.0),
        "cache_kidx": nrm(ks[7], (DEPTH, n_phys, PAGE_SIZE, IDX_DIM), 1.0),
        "state_conv": nrm(ks[8], (DEPTH, DEC_BATCH, CONV_WIDTH - 1, CONV_DIM), 1.0),
        "page_table": page_table,
        "norm_g": 1.0 + nrm(ks[9], (DEPTH, D_MODEL), 0.05),
        "w_mod": nrm(ks[10], (DEPTH, D_MODEL, 3 * D_MODEL), 0.5 * D_MODEL ** -0.5),
        "b_mod": nrm(ks[11], (DEPTH, 3 * D_MODEL), 0.02),
        "w_in": nrm(ks[12], (DEPTH, D_MODEL, p_cols), D_MODEL ** -0.5),
        "conv_w": nrm(ks[13], (DEPTH, CONV_WIDTH, CONV_DIM), CONV_WIDTH ** -0.5),
        "q_norm": 1.0 + nrm(ks[14], (DEPTH, HEAD_DIM), 0.05),
        "k_norm": 1.0 + nrm(ks[15], (DEPTH, HEAD_DIM), 0.05),
        "w_proj_a": nrm(ks[16], (DEPTH, CONV_DIM, D_MODEL), CONV_DIM ** -0.5),
        "w_proj_b": nrm(ks[17], (DEPTH, ATT_DIM, D_MODEL), ATT_DIM ** -0.5),
        "w_out": nrm(ks[18], (DEPTH, D_MODEL, D_MODEL), D_MODEL ** -0.5),
    }


def reference(x_prompt, x_sample, c_prompt, c_sample, cache_k, cache_v, cache_kidx, state_conv, page_table,
              norm_g, w_mod, b_mod, w_in, conv_w, q_norm, k_norm, w_proj_a, w_proj_b, w_out):
    yp, ys = x_prompt, x_sample
    kp, vp, kip, cp = [], [], [], []
    kss, vss, kis, css = [], [], [], []
    for l in range(DEPTH):
        wl = (norm_g[l], w_mod[l], b_mod[l], w_in[l], conv_w[l], q_norm[l], k_norm[l],
              w_proj_a[l], w_proj_b[l], w_out[l])
        prev0 = jnp.zeros((yp.shape[0], CONV_WIDTH - 1, CONV_DIM), yp.dtype)
        yp, k1, v1, ki1, c1 = _layer(yp, c_prompt, prev0, _prompt_attention, *wl)
        kp.append(k1); vp.append(v1); kip.append(ki1); cp.append(c1)
        pk, pv, pki = cache_k[l], cache_v[l], cache_kidx[l]

        def attend_s(q, qi, wi, k, v, ki, pk=pk, pv=pv, pki=pki):
            return _sample_attention(q, qi, wi, k, v, ki, pk, pv, pki, page_table)

        ys, k2, v2, ki2, c2 = _layer(ys, c_sample, state_conv[l], attend_s, *wl)
        kss.append(k2); vss.append(v2); kis.append(ki2); css.append(c2)
    return (yp, ys, jnp.stack(kp), jnp.stack(vp), jnp.stack(kip), jnp.stack(cp),
            jnp.stack(kss), jnp.stack(vss), jnp.stack(kis), jnp.stack(css))
```

```python
import functools

import jax
import jax.numpy as jnp
from jax import lax
from jax.experimental import pallas as pl
from jax.experimental.pallas import tpu as pltpu

F32 = jnp.float32
BF16 = jnp.bfloat16

D_MODEL = 1024
CONV_DIM = 1024
CONV_WIDTH = 3
N_HEADS = 8
N_KV_HEADS = 2
HEAD_DIM = 128
ATT_DIM = N_HEADS * HEAD_DIM
KV_DIM = N_KV_HEADS * HEAD_DIM
GROUP = N_HEADS // N_KV_HEADS
N_IDX_HEADS = 8
IDX_DIM = 64
QI_DIM = N_IDX_HEADS * IDX_DIM
TOPK_MAX = 256
EPS = 1e-6
MASKED_LOGIT = -1e30

LANES = 128
SUBLANES = 8
V7X_VMEM_LIMIT_BYTES = 56 * 1024 * 1024

OFF_UH = 0
OFF_UB = OFF_UH + CONV_DIM
OFF_UC = OFF_UB + CONV_DIM
OFF_ZA = OFF_UC + CONV_DIM
OFF_Q = OFF_ZA + CONV_DIM
OFF_K = OFF_Q + ATT_DIM
OFF_V = OFF_K + KV_DIM
OFF_QI = OFF_V + KV_DIM
OFF_ZB = OFF_QI + QI_DIM
OFF_GA = OFF_ZB + ATT_DIM
OFF_GB = OFF_GA + D_MODEL
OFF_KW = OFF_GB + D_MODEL
P_COLS = OFF_KW + LANES

KEY_LOWEST_FINITE = -2139095040
INT_MIN = -(2 ** 31)

_NT = (((1,), (1,)), ((), ()))


def _silu(x):
    return x * jax.nn.sigmoid(x)


def _rms_rows(x, g):
    ms = jnp.mean(x * x, axis=-1, keepdims=True)
    return (x * lax.rsqrt(ms + EPS)) * g


def _mod_kernel(c_ref, w_ref, b_ref, o_ref):
    s = _silu(c_ref[...]).astype(BF16)
    o_ref[0] = jnp.dot(s, w_ref[0].astype(BF16), preferred_element_type=F32) + b_ref[0]


def _modulation(c_all, w_mod, b_mod):
    depth = w_mod.shape[0]
    nb = c_all.shape[0]
    return pl.pallas_call(
        _mod_kernel,
        grid=(depth,),
        in_specs=[
            pl.BlockSpec((nb, D_MODEL), lambda l: (0, 0)),
            pl.BlockSpec((1, D_MODEL, 3 * D_MODEL), lambda l: (l, 0, 0)),
            pl.BlockSpec((1, 1, 3 * D_MODEL), lambda l: (l, 0, 0)),
        ],
        out_specs=pl.BlockSpec((1, nb, 3 * D_MODEL), lambda l: (l, 0, 0)),
        out_shape=jax.ShapeDtypeStruct((depth, nb, 3 * D_MODEL), F32),
        compiler_params=pltpu.CompilerParams(
            dimension_semantics=("arbitrary",), vmem_limit_bytes=V7X_VMEM_LIMIT_BYTES),
        name="adaln_mod",
    )(c_all, w_mod, b_mod.reshape(depth, 1, 3 * D_MODEL))


def _proj_kernel(*refs, seg_len):
    if seg_len is None:
        (x_ref, ng_ref, sc_ref, sh_ref, w_ref, cw_ref, qn_ref, kn_ref, wpa_ref,
         ap_ref, qb_ref, kf_ref, kb_ref, vf_ref, vb_ref, qib_ref, kif_ref, kib_ref, wq_ref,
         zb_ref, sgb_ref, ust_ref, carry_ref) = refs
    else:
        (x_ref, ng_ref, sc_ref, sh_ref, w_ref, cw_ref, qn_ref, kn_ref, wpa_ref, p1_ref, p2_ref,
         ap_ref, qb_ref, kf_ref, kb_ref, vf_ref, vb_ref, qib_ref, kif_ref, kib_ref, wq_ref,
         zb_ref, sgb_ref, ust_ref) = refs
    tm = x_ref.shape[0]

    h = _rms_rows(x_ref[...], ng_ref[...]) * (1.0 + sc_ref[...]) + sh_ref[...]
    hb = h.astype(BF16)

    def proj(off, width):
        return jnp.dot(hb, w_ref[:, off:off + width], preferred_element_type=F32)

    u = proj(OFF_UC, CONV_DIM) * proj(OFF_UH, CONV_DIM)
    row = lax.broadcasted_iota(jnp.int32, (tm, 1), 0)
    if seg_len is None:
        @pl.when(pl.program_id(0) == 0)
        def _():
            carry_ref[...] = jnp.zeros_like(carry_ref)
        prev2 = carry_ref[SUBLANES - 2:SUBLANES - 1, :]
        prev1 = carry_ref[SUBLANES - 1:SUBLANES, :]
        pos = row
        p1 = prev1
        p2 = jnp.where(pos == 0, prev2, prev1)
    else:
        pos = row % seg_len
        p1 = p1_ref[...]
        p2 = p2_ref[...]
    u1 = jnp.where(pos == 0, p1, pltpu.roll(u, 1, 0))
    u2 = jnp.where(pos < 2, p2, pltpu.roll(u, 2, 0))
    conv = (cw_ref[0:1, :] * u2 + cw_ref[1:2, :] * u1) + cw_ref[2:3, :] * u
    if seg_len is None:
        carry_ref[...] = u[tm - SUBLANES:tm, :]
        ust_ref[...] = u[tm - SUBLANES:tm, :]
    else:
        ust_ref[...] = u
    branch_a = (proj(OFF_UB, CONV_DIM) * conv) * _silu(proj(OFF_ZA, CONV_DIM))
    ap_ref[...] = jax.nn.sigmoid(proj(OFF_GA, D_MODEL)) * jnp.dot(
        branch_a.astype(BF16), wpa_ref[...], preferred_element_type=F32)

    qf = proj(OFF_Q, ATT_DIM)
    for hd in range(N_HEADS):
        sl = slice(hd * HEAD_DIM, (hd + 1) * HEAD_DIM)
        qb_ref[:, sl] = (_rms_rows(qf[:, sl], qn_ref[...]) * (HEAD_DIM ** -0.5)).astype(BF16)
    kf = proj(OFF_K, KV_DIM)
    for g in range(N_KV_HEADS):
        sl = slice(g * HEAD_DIM, (g + 1) * HEAD_DIM)
        kn = _rms_rows(kf[:, sl], kn_ref[...])
        kf_ref[:, sl] = kn
        kb_ref[:, sl] = kn.astype(BF16)
    vf = proj(OFF_V, KV_DIM)
    vf_ref[...] = vf
    vb_ref[...] = vf.astype(BF16)
    qib_ref[...] = proj(OFF_QI, QI_DIM).astype(BF16)
    kw = proj(OFF_KW, LANES)
    kif_ref[...] = kw[:, 0:IDX_DIM]
    kib_ref[...] = kw[:, 0:IDX_DIM].astype(BF16)
    wq_ref[...] = (kw[:, IDX_DIM:IDX_DIM + N_IDX_HEADS] * (N_IDX_HEADS ** -0.5)) * (IDX_DIM ** -0.5)
    zb_ref[...] = _silu(proj(OFF_ZB, ATT_DIM))
    sgb_ref[...] = jax.nn.sigmoid(proj(OFF_GB, D_MODEL))


def _row_tile(t):
    for tm in (512, 256, 128, 64, 32, 16, 8):
        if t % tm == 0:
            return tm
    raise ValueError(f"row count {t} is not a multiple of {SUBLANES}")


def _projection(x, ng, scale, shift, w_in_p, conv_w, qn, kn, w_pa, prev=None, seg_len=None):
    t = x.shape[0]
    tm = _row_tile(t) if seg_len is None else t
    gr = scale.shape[0]
    assert gr in (1, t)
    row = lambda i: (i, 0)
    const = lambda i: (0, 0)
    mod_spec = pl.BlockSpec((1 if gr == 1 else tm, D_MODEL), const if gr == 1 else row)
    in_specs = [
        pl.BlockSpec((tm, D_MODEL), row),
        pl.BlockSpec((1, D_MODEL), const),
        mod_spec, mod_spec,
        pl.BlockSpec((D_MODEL, P_COLS), const),
        pl.BlockSpec((CONV_WIDTH, CONV_DIM), const),
        pl.BlockSpec((1, HEAD_DIM), const),
        pl.BlockSpec((1, HEAD_DIM), const),
        pl.BlockSpec((CONV_DIM, D_MODEL), const),
    ]
    args = [x, ng, scale, shift, w_in_p, conv_w, qn, kn, w_pa]
    scratch = []
    if seg_len is None:
        ust_shape, ust_spec = (SUBLANES, CONV_DIM), pl.BlockSpec((SUBLANES, CONV_DIM), const)
        scratch.append(pltpu.VMEM((SUBLANES, CONV_DIM), F32))
    else:
        in_specs += [pl.BlockSpec((tm, CONV_DIM), row)] * 2
        args += list(prev)
        ust_shape, ust_spec = (t, CONV_DIM), pl.BlockSpec((tm, CONV_DIM), row)
    widths = [(D_MODEL, F32), (ATT_DIM, BF16), (KV_DIM, F32), (KV_DIM, BF16), (KV_DIM, F32),
              (KV_DIM, BF16), (QI_DIM, BF16), (IDX_DIM, F32), (IDX_DIM, BF16), (N_IDX_HEADS, F32),
              (ATT_DIM, F32), (D_MODEL, F32)]
    out_shape = [jax.ShapeDtypeStruct((t, w), dt) for w, dt in widths]
    out_specs = [pl.BlockSpec((tm, w), row) for w, _ in widths]
    out_shape.append(jax.ShapeDtypeStruct(ust_shape, F32))
    out_specs.append(ust_spec)
    return pl.pallas_call(
        functools.partial(_proj_kernel, seg_len=seg_len),
        grid=(t // tm,),
        in_specs=in_specs,
        out_specs=out_specs,
        out_shape=out_shape,
        scratch_shapes=scratch,
        compiler_params=pltpu.CompilerParams(
            dimension_semantics=("arbitrary",), vmem_limit_bytes=V7X_VMEM_LIMIT_BYTES),
        name="proj_prompt" if seg_len is None else "proj_sample",
    )(*args)


def _out_kernel(ap_ref, at_ref, zb_ref, sgb_ref, x_ref, gate_ref, wpb_ref, wout_ref, y_ref):
    branch_b = (at_ref[...] * zb_ref[...]).astype(BF16)
    merged = ap_ref[...] + sgb_ref[...] * jnp.dot(branch_b, wpb_ref[...], preferred_element_type=F32)
    y_ref[...] = x_ref[...] + gate_ref[...] * jnp.dot(
        merged.astype(BF16), wout_ref[...], preferred_element_type=F32)


def _output(a_part, attn, zb, sgb, x, gate, w_pb, w_out):
    t = x.shape[0]
    tm = _row_tile(t)
    gr = gate.shape[0]
    assert gr in (1, t)
    row = lambda i: (i, 0)
    const = lambda i: (0, 0)
    act = pl.BlockSpec((tm, D_MODEL), row)
    return pl.pallas_call(
        _out_kernel,
        grid=(t // tm,),
        in_specs=[act, act, act, act, act,
                  pl.BlockSpec((1 if gr == 1 else tm, D_MODEL), const if gr == 1 else row),
                  pl.BlockSpec((ATT_DIM, D_MODEL), const),
                  pl.BlockSpec((D_MODEL, D_MODEL), const)],
        out_specs=act,
        out_shape=jax.ShapeDtypeStruct((t, D_MODEL), F32),
        compiler_params=pltpu.CompilerParams(
            dimension_semantics=("arbitrary",), vmem_limit_bytes=V7X_VMEM_LIMIT_BYTES),
        name="out_proj",
    )(a_part, attn, zb, sgb, x, gate, w_pb, w_out)


def _key_to_float(key):
    bits = jnp.where(key >= 0, key, key ^ jnp.int32(0x7FFFFFFF))
    return lax.bitcast_convert_type(bits, F32)


def _count_ge(segments, thr):
    rows = thr.shape[0]
    thr_b = jnp.broadcast_to(thr, (rows, LANES))
    total = jnp.zeros((rows, LANES), F32)
    for ref, n_tiles in segments:
        width = ref.shape[2]

        def body(j, cnt, ref=ref, width=width):
            tile = ref[j]
            for c in range(width // LANES):
                cnt = cnt + jnp.where(tile[:, c * LANES:(c + 1) * LANES] >= thr_b, 1.0, 0.0)
            return cnt

        total = lax.fori_loop(0, n_tiles, body, total)
    return jnp.sum(total, axis=-1, keepdims=True)


def _kth_largest(segments, rows, k):
    kf = float(k)

    def body(i, carry):
        prefix, cge = carry
        bit = jnp.left_shift(jnp.int32(1), 31 - i)
        cand = prefix ^ bit
        cnt = _count_ge(segments, _key_to_float(cand))
        take = cnt >= kf
        return jnp.where(take, cand, prefix), jnp.where(take, cnt, cge)

    prefix0 = jnp.full((rows, 1), INT_MIN, jnp.int32)
    prefix, cge = lax.fori_loop(0, 32, body, (prefix0, jnp.zeros((rows, 1), F32)))
    prefix = jnp.maximum(prefix, jnp.int32(KEY_LOWEST_FINITE))
    return _key_to_float(prefix), cge


def _drop_excess_ties(segments, thr, cge, k):
    rows = thr.shape[0]
    excess = cge > float(k)
    n_eq = jnp.zeros((rows, 1), F32)
    for ref, n_tiles in segments:
        def count_body(j, acc, ref=ref):
            return acc + jnp.sum(jnp.where(ref[j] == thr, 1.0, 0.0), axis=-1, keepdims=True)
        n_eq = lax.fori_loop(0, n_tiles, count_body, n_eq)
    need = float(k) - (cge - n_eq)
    seen = jnp.zeros((rows, 1), F32)
    for ref, n_tiles in segments:
        width = ref.shape[2]
        cw = min(width, 2 * LANES)
        r_i = lax.broadcasted_iota(jnp.int32, (cw, cw), 0)
        c_i = lax.broadcasted_iota(jnp.int32, (cw, cw), 1)
        tri = jnp.where(r_i <= c_i, 1.0, 0.0).astype(BF16)

        def drop_body(j, seen, ref=ref, width=width, cw=cw, tri=tri):
            for c in range(width // cw):
                tile = ref[j, :, c * cw:(c + 1) * cw]
                eq = tile == thr
                ordinal = seen + jnp.dot(jnp.where(eq, 1.0, 0.0).astype(BF16), tri,
                                         preferred_element_type=F32)
                drop = eq & excess & (ordinal > need)
                ref[j, :, c * cw:(c + 1) * cw] = jnp.where(drop, -jnp.inf, tile)
                seen = seen + jnp.sum(jnp.where(eq, 1.0, 0.0), axis=-1, keepdims=True)
            return seen

        seen = lax.fori_loop(0, n_tiles, drop_body, seen)


def _select_threshold(segments, rows, k):
    thr, cge = _kth_largest(segments, rows, k)

    @pl.when(jnp.max(cge) > float(k))
    def _():
        _drop_excess_ties(segments, thr, cge, k)

    return thr


def _index_scores(qi_heads, w_cols, kt):
    acc = None
    for qh, wh in zip(qi_heads, w_cols):
        s = lax.dot_general(qh, kt, _NT, preferred_element_type=F32)
        term = jnp.maximum(s, 0.0) * wh
        acc = term if acc is None else acc + term
    return acc + 0.0


def _softmax_step(q, kt, vt, sel, m_old, l_old, acc_old):
    s = lax.dot_general(q, kt, _NT, preferred_element_type=F32)
    s = jnp.where(sel, s, MASKED_LOGIT)
    m_new = jnp.maximum(m_old, jnp.max(s, axis=-1, keepdims=True))
    p = jnp.exp(s - m_new)
    alpha = jnp.exp(m_old - m_new)
    l_new = alpha * l_old + jnp.sum(p, axis=-1, keepdims=True)
    acc_new = alpha * acc_old + jnp.dot(p.astype(BF16), vt, preferred_element_type=F32)
    return m_new, l_new, acc_new


def _prompt_attn_kernel(q_ref, qi_ref, wq_ref, k_ref, v_ref, ki_ref, o_ref,
                        s_ref, m_ref, l_ref, acc_ref, *, top_k):
    tq = q_ref.shape[0]
    tk = s_ref.shape[2]
    i = pl.program_id(0)

    qi = qi_ref[...]
    wq = wq_ref[...]
    qi_heads = [qi[:, h * IDX_DIM:(h + 1) * IDX_DIM] for h in range(N_IDX_HEADS)]
    w_cols = [wq[:, h:h + 1] for h in range(N_IDX_HEADS)]

    def score_body(j, carry):
        kt = ki_ref[pl.ds(pl.multiple_of(j * tk, tk), tk), :]
        s_ref[j] = _index_scores(qi_heads, w_cols, kt)
        return carry

    lax.fori_loop(0, i, score_body, 0)
    kt = ki_ref[pl.ds(pl.multiple_of(i * tk, tk), tk), :]
    visible = (lax.broadcasted_iota(jnp.int32, (tq, tk), 1)
               <= lax.broadcasted_iota(jnp.int32, (tq, tk), 0))
    s_ref[i] = jnp.where(visible, _index_scores(qi_heads, w_cols, kt), -jnp.inf)

    thr = _select_threshold([(s_ref, i + 1)], tq, top_k)

    m_ref[...] = jnp.full(m_ref.shape, MASKED_LOGIT, F32)
    l_ref[...] = jnp.zeros(l_ref.shape, F32)
    acc_ref[...] = jnp.zeros(acc_ref.shape, F32)

    def attend_body(j, carry):
        rows = pl.ds(pl.multiple_of(j * tk, tk), tk)
        sel = s_ref[j] >= thr
        for h in range(N_HEADS):
            g = h // GROUP
            kv_cols = slice(g * HEAD_DIM, (g + 1) * HEAD_DIM)
            m_new, l_new, acc_new = _softmax_step(
                q_ref[:, h * HEAD_DIM:(h + 1) * HEAD_DIM], k_ref[rows, kv_cols], v_ref[rows, kv_cols],
                sel, m_ref[h], l_ref[h], acc_ref[h])
            m_ref[h] = m_new
            l_ref[h] = l_new
            acc_ref[h] = acc_new
        return carry

    lax.fori_loop(0, i + 1, attend_body, 0)
    for h in range(N_HEADS):
        o_ref[:, h * HEAD_DIM:(h + 1) * HEAD_DIM] = acc_ref[h] * (1.0 / l_ref[h])


def _prompt_attention(qb, qib, wq, kb, vb, kib):
    t = qb.shape[0]
    tq = min(2 * LANES, t)
    assert t % tq == 0
    top_k = min(TOPK_MAX, t // 4)
    row = lambda i: (i, 0)
    const = lambda i: (0, 0)
    return pl.pallas_call(
        functools.partial(_prompt_attn_kernel, top_k=top_k),
        grid=(t // tq,),
        in_specs=[
            pl.BlockSpec((tq, ATT_DIM), row),
            pl.BlockSpec((tq, QI_DIM), row),
            pl.BlockSpec((tq, N_IDX_HEADS), row),
            pl.BlockSpec((t, KV_DIM), const),
            pl.BlockSpec((t, KV_DIM), const),
            pl.BlockSpec((t, IDX_DIM), const),
        ],
        out_specs=pl.BlockSpec((tq, ATT_DIM), row),
        out_shape=jax.ShapeDtypeStruct((t, ATT_DIM), F32),
        scratch_shapes=[
            pltpu.VMEM((t // tq, tq, tq), F32),
            pltpu.VMEM((N_HEADS, tq, 1), F32),
            pltpu.VMEM((N_HEADS, tq, 1), F32),
            pltpu.VMEM((N_HEADS, tq, HEAD_DIM), F32),
        ],
        compiler_params=pltpu.CompilerParams(
            dimension_semantics=("arbitrary",), vmem_limit_bytes=V7X_VMEM_LIMIT_BYTES),
        name="attn_prompt",
    )(qb, qib, wq, kb, vb, kib)


def _sample_attn_kernel(pt_ref, qi_ref, wq_ref, q_ref, kin_ref, kn_ref, vn_ref,
                        ck_hbm, cv_hbm, cki_hbm, o_ref,
                        kbuf, vbuf, kibuf, s_main, s_tail, sem, *, layer, top_k, pages_per_chunk):
    b = pl.program_id(0)
    n_chunks, n_tok, chunk = s_main.shape
    page = chunk // pages_per_chunk

    def chunk_copies(hbm, buf, sem_row, c, slot):
        return [pltpu.make_async_copy(hbm.at[layer, pt_ref[b, c * pages_per_chunk + p]],
                                      buf.at[slot, pl.ds(p * page, page)], sem.at[sem_row, slot])
                for p in range(pages_per_chunk)]

    def start(copies):
        for cp in copies:
            cp.start()

    def wait(copies):
        for cp in copies:
            cp.wait()

    qi = qi_ref[0]
    wq = wq_ref[0]
    w_cols = [wq[:, h:h + 1] for h in range(N_IDX_HEADS)]

    def scores(kt):
        s_all = lax.dot_general(qi, kt, _NT, preferred_element_type=F32)
        acc = None
        for h in range(N_IDX_HEADS):
            term = jnp.maximum(s_all[h * n_tok:(h + 1) * n_tok, :], 0.0) * w_cols[h]
            acc = term if acc is None else acc + term
        return acc + 0.0

    start(chunk_copies(cki_hbm, kibuf, 0, 0, 0))

    def score_body(c, carry):
        slot = c % 2

        @pl.when(c + 1 < n_chunks)
        def _():
            start(chunk_copies(cki_hbm, kibuf, 0, c + 1, 1 - slot))

        wait(chunk_copies(cki_hbm, kibuf, 0, c, slot))
        s_main[c] = scores(kibuf[slot].astype(BF16))
        return carry

    lax.fori_loop(0, n_chunks, score_body, 0)
    col = lax.broadcasted_iota(jnp.int32, (n_tok, LANES), 1)
    tok = lax.broadcasted_iota(jnp.int32, (n_tok, LANES), 0)
    s_tail[0] = jnp.where(col <= tok, scores(kin_ref[0]), -jnp.inf)

    start(chunk_copies(ck_hbm, kbuf, 1, 0, 0))
    start(chunk_copies(cv_hbm, vbuf, 2, 0, 0))

    thr = _select_threshold([(s_main, n_chunks), (s_tail, 1)], n_tok, top_k)

    rows = GROUP * n_tok

    thr_rows = jnp.concatenate([thr] * GROUP, axis=0)

    def attend(state, kt, vt, s_tok):
        sel = jnp.concatenate([s_tok] * GROUP, axis=0) >= thr_rows
        new_state = []
        for g in range(N_KV_HEADS):
            cols = slice(g * HEAD_DIM, (g + 1) * HEAD_DIM)
            new_state.append(_softmax_step(q_ref[0, g], kt[:, cols], vt[:, cols], sel, *state[g]))
        return tuple(new_state)

    def attend_body(c, state):
        slot = c % 2

        @pl.when(c + 1 < n_chunks)
        def _():
            start(chunk_copies(ck_hbm, kbuf, 1, c + 1, 1 - slot))
            start(chunk_copies(cv_hbm, vbuf, 2, c + 1, 1 - slot))

        wait(chunk_copies(ck_hbm, kbuf, 1, c, slot))
        wait(chunk_copies(cv_hbm, vbuf, 2, c, slot))
        return attend(state, kbuf[slot].astype(BF16), vbuf[slot].astype(BF16), s_main[c])

    init = tuple((jnp.full((rows, 1), MASKED_LOGIT, F32), jnp.zeros((rows, 1), F32),
                  jnp.zeros((rows, HEAD_DIM), F32)) for _ in range(N_KV_HEADS))
    state = lax.fori_loop(0, n_chunks, attend_body, init)
    state = attend(state, kn_ref[0], vn_ref[0], s_tail[0])
    for g in range(N_KV_HEADS):
        _, l_fin, acc_fin = state[g]
        o_ref[0, g] = acc_fin * (1.0 / l_fin)


def _sample_attention(qb, qib, wq, kb, vb, kib, cache_k, cache_v, cache_kidx, page_table, layer):
    nb, n_pages = page_table.shape
    n_tok = qb.shape[0] // nb
    page = cache_k.shape[2]
    assert n_tok == SUBLANES and n_tok <= LANES
    pages_per_chunk = 16
    while n_pages % pages_per_chunk:
        pages_per_chunk //= 2
    n_chunks = n_pages // pages_per_chunk
    chunk = pages_per_chunk * page
    top_k = min(TOPK_MAX, (n_pages * page + n_tok) // 4)

    q_s = qb.reshape(nb, n_tok, N_KV_HEADS, GROUP, HEAD_DIM).transpose(0, 2, 3, 1, 4)
    q_s = q_s.reshape(nb, N_KV_HEADS, GROUP * n_tok, HEAD_DIM)
    qi_s = qib.reshape(nb, n_tok, N_IDX_HEADS, IDX_DIM).transpose(0, 2, 1, 3)
    qi_s = qi_s.reshape(nb, N_IDX_HEADS * n_tok, IDX_DIM)
    wq_s = wq.reshape(nb, n_tok, N_IDX_HEADS)
    pad = lambda a: jnp.pad(a.reshape(nb, n_tok, a.shape[-1]), ((0, 0), (0, LANES - n_tok), (0, 0)))
    kin_s, kn_s, vn_s = pad(kib), pad(kb), pad(vb)

    seq3 = lambda b, pt: (b, 0, 0)
    seq4 = lambda b, pt: (b, 0, 0, 0)
    out = pl.pallas_call(
        functools.partial(_sample_attn_kernel, layer=layer, top_k=top_k,
                          pages_per_chunk=pages_per_chunk),
        grid_spec=pltpu.PrefetchScalarGridSpec(
            num_scalar_prefetch=1,
            grid=(nb,),
            in_specs=[
                pl.BlockSpec((1, N_IDX_HEADS * n_tok, IDX_DIM), seq3),
                pl.BlockSpec((1, n_tok, N_IDX_HEADS), seq3),
                pl.BlockSpec((1, N_KV_HEADS, GROUP * n_tok, HEAD_DIM), seq4),
                pl.BlockSpec((1, LANES, IDX_DIM), seq3),
                pl.BlockSpec((1, LANES, KV_DIM), seq3),
                pl.BlockSpec((1, LANES, KV_DIM), seq3),
                pl.BlockSpec(memory_space=pl.ANY),
                pl.BlockSpec(memory_space=pl.ANY),
                pl.BlockSpec(memory_space=pl.ANY),
            ],
            out_specs=pl.BlockSpec((1, N_KV_HEADS, GROUP * n_tok, HEAD_DIM), seq4),
            scratch_shapes=[
                pltpu.VMEM((2, chunk, KV_DIM), F32),
                pltpu.VMEM((2, chunk, KV_DIM), F32),
                pltpu.VMEM((2, chunk, IDX_DIM), F32),
                pltpu.VMEM((n_chunks, n_tok, chunk), F32),
                pltpu.VMEM((1, n_tok, LANES), F32),
                pltpu.SemaphoreType.DMA((3, 2)),
            ],
        ),
        out_shape=jax.ShapeDtypeStruct((nb, N_KV_HEADS, GROUP * n_tok, HEAD_DIM), F32),
        compiler_params=pltpu.CompilerParams(
            dimension_semantics=("arbitrary",), vmem_limit_bytes=V7X_VMEM_LIMIT_BYTES),
        name="attn_sample",
    )(page_table, qi_s, wq_s, q_s, kin_s, kn_s, vn_s, cache_k, cache_v, cache_kidx)
    out = out.reshape(nb, N_KV_HEADS, GROUP, n_tok, HEAD_DIM).transpose(0, 3, 1, 2, 4)
    return out.reshape(nb * n_tok, ATT_DIM)


def _pack_w_in(w):
    ends, off = [], 0
    for size in (CONV_DIM, CONV_DIM, CONV_DIM, CONV_DIM, ATT_DIM, KV_DIM, KV_DIM, QI_DIM, IDX_DIM,
                 N_IDX_HEADS, ATT_DIM, D_MODEL, D_MODEL):
        ends.append((off, off + size))
        off += size
    (uh, ub, uc, za, q, k, v, qi, ki, wi, zb, ga, gb) = [w[:, a:b] for a, b in ends]
    pad = jnp.zeros((w.shape[0], LANES - IDX_DIM - N_IDX_HEADS), w.dtype)
    return jnp.concatenate([uh, ub, uc, za, q, k, v, qi, zb, ga, gb, ki, wi, pad], axis=1).astype(BF16)


def kernel(x_prompt, x_sample, c_prompt, c_sample, cache_k, cache_v, cache_kidx, state_conv, page_table,
           norm_g, w_mod, b_mod, w_in, conv_w, q_norm, k_norm, w_proj_a, w_proj_b, w_out):
    depth = w_in.shape[0]
    bp, seq, _ = x_prompt.shape
    nb, n_tok, _ = x_sample.shape
    assert bp == 1
    n_phys, page = cache_k.shape[1], cache_k.shape[2]
    cache_k = cache_k.reshape(depth, n_phys, page, KV_DIM)
    cache_v = cache_v.reshape(depth, n_phys, page, KV_DIM)

    mod = _modulation(jnp.concatenate([c_prompt, c_sample], axis=0), w_mod, b_mod)

    yp = x_prompt.reshape(seq, D_MODEL)
    ys = x_sample.reshape(nb * n_tok, D_MODEL)
    outs = [[] for _ in range(8)]
    for l in range(depth):
        w_in_p = _pack_w_in(w_in[l])
        w_pa, w_pb, w_o = (w_proj_a[l].astype(BF16), w_proj_b[l].astype(BF16), w_out[l].astype(BF16))
        ng, qn, kn = norm_g[l][None, :], q_norm[l][None, :], k_norm[l][None, :]

        shift, scale, gate = jnp.split(mod[l, 0:bp], 3, axis=-1)
        (ap, qb, kf, kb, vf, vb, qib, kif, kib, wq, zb, sgb, ust) = _projection(
            yp, ng, scale, shift, w_in_p, conv_w[l], qn, kn, w_pa)
        attn = _prompt_attention(qb, qib, wq, kb, vb, kib)
        yp = _output(ap, attn, zb, sgb, yp, gate, w_pb, w_o)
        outs[0].append(kf.reshape(bp, seq, N_KV_HEADS, HEAD_DIM))
        outs[1].append(vf.reshape(bp, seq, N_KV_HEADS, HEAD_DIM))
        outs[2].append(kif.reshape(bp, seq, IDX_DIM))
        outs[3].append(ust[SUBLANES - (CONV_WIDTH - 1):].reshape(bp, CONV_WIDTH - 1, CONV_DIM))

        shift, scale, gate = [jnp.repeat(m, n_tok, axis=0) for m in jnp.split(mod[l, bp:], 3, axis=-1)]
        st = state_conv[l]
        zeros = jnp.zeros((nb, n_tok - 2, CONV_DIM), F32)
        p1 = jnp.concatenate([st[:, 1:2], st[:, 1:2], zeros], axis=1).reshape(nb * n_tok, CONV_DIM)
        p2 = jnp.concatenate([st[:, 0:1], st[:, 1:2], zeros], axis=1).reshape(nb * n_tok, CONV_DIM)
        (ap, qb, kf, kb, vf, vb, qib, kif, kib, wq, zb, sgb, ust) = _projection(
            ys, ng, scale, shift, w_in_p, conv_w[l], qn, kn, w_pa, prev=(p1, p2), seg_len=n_tok)
        attn = _sample_attention(qb, qib, wq, kb, vb, kib, cache_k, cache_v, cache_kidx, page_table, l)
        ys = _output(ap, attn, zb, sgb, ys, gate, w_pb, w_o)
        outs[4].append(kf.reshape(nb, n_tok, N_KV_HEADS, HEAD_DIM))
        outs[5].append(vf.reshape(nb, n_tok, N_KV_HEADS, HEAD_DIM))
        outs[6].append(kif.reshape(nb, n_tok, IDX_DIM))
        outs[7].append(ust.reshape(nb, n_tok, CONV_DIM)[:, n_tok - (CONV_WIDTH - 1):])

    return (yp.reshape(bp, seq, D_MODEL), ys.reshape(nb, n_tok, D_MODEL),
            *[jnp.stack(o) for o in outs])
```

```python
import functools

import jax
import jax.numpy as jnp
from jax import lax
from jax.experimental import pallas as pl
from jax.experimental.pallas import tpu as pltpu

F32 = jnp.float32
BF16 = jnp.bfloat16

D_MODEL = 1024
CONV_DIM = 1024
CONV_WIDTH = 3
N_HEADS = 8
N_KV_HEADS = 2
HEAD_DIM = 128
ATT_DIM = N_HEADS * HEAD_DIM
KV_DIM = N_KV_HEADS * HEAD_DIM
GROUP = N_HEADS // N_KV_HEADS
N_IDX_HEADS = 8
IDX_DIM = 64
QI_DIM = N_IDX_HEADS * IDX_DIM
TOPK_MAX = 256
EPS = 1e-6
MASKED_LOGIT = -1e30
LOG2_E = 1.4426950408889634

LANES = 128
SUBLANES = 8
V7X_VMEM_LIMIT_BYTES = 56 * 1024 * 1024

OFF_UH = 0
OFF_UB = OFF_UH + CONV_DIM
OFF_UC = OFF_UB + CONV_DIM
OFF_ZA = OFF_UC + CONV_DIM
OFF_Q = OFF_ZA + CONV_DIM
OFF_K = OFF_Q + ATT_DIM
OFF_V = OFF_K + KV_DIM
OFF_QI = OFF_V + KV_DIM
OFF_ZB = OFF_QI + QI_DIM
OFF_GA = OFF_ZB + ATT_DIM
OFF_GB = OFF_GA + D_MODEL
OFF_KW = OFF_GB + D_MODEL
P_COLS = OFF_KW + LANES

KEY_LOWEST_FINITE = -2139095040
INT_MIN = -(2 ** 31)

_NT = (((1,), (1,)), ((), ()))


def _silu(x):
    return x * jax.nn.sigmoid(x)


def _rms_rows(x, g):
    ms = jnp.mean(x * x, axis=-1, keepdims=True)
    return (x * lax.rsqrt(ms + EPS)) * g


def _mod_kernel(c_ref, w_ref, b_ref, o_ref):
    s = _silu(c_ref[...]).astype(BF16)
    o_ref[0] = jnp.dot(s, w_ref[0].astype(BF16), preferred_element_type=F32) + b_ref[0]


def _modulation(c_all, w_mod, b_mod):
    depth = w_mod.shape[0]
    nb = c_all.shape[0]
    return pl.pallas_call(
        _mod_kernel,
        grid=(depth,),
        in_specs=[
            pl.BlockSpec((nb, D_MODEL), lambda l: (0, 0)),
            pl.BlockSpec((1, D_MODEL, 3 * D_MODEL), lambda l: (l, 0, 0)),
            pl.BlockSpec((1, 1, 3 * D_MODEL), lambda l: (l, 0, 0)),
        ],
        out_specs=pl.BlockSpec((1, nb, 3 * D_MODEL), lambda l: (l, 0, 0)),
        out_shape=jax.ShapeDtypeStruct((depth, nb, 3 * D_MODEL), F32),
        compiler_params=pltpu.CompilerParams(
            dimension_semantics=("arbitrary",), vmem_limit_bytes=V7X_VMEM_LIMIT_BYTES),
        name="adaln_mod",
    )(c_all, w_mod, b_mod.reshape(depth, 1, 3 * D_MODEL))


def _proj_kernel(*refs, seg_len):
    if seg_len is None:
        (x_ref, ng_ref, sc_ref, sh_ref, w_ref, cw_ref, qn_ref, kn_ref, wpa_ref,
         ap_ref, qb_ref, kf_ref, kb_ref, vf_ref, vb_ref, qib_ref, kif_ref, kib_ref, wq_ref,
         zb_ref, sgb_ref, ust_ref, carry_ref) = refs
    else:
        (x_ref, ng_ref, sc_ref, sh_ref, w_ref, cw_ref, qn_ref, kn_ref, wpa_ref, p1_ref, p2_ref,
         ap_ref, qb_ref, kf_ref, kb_ref, vf_ref, vb_ref, qib_ref, kif_ref, kib_ref, wq_ref,
         zb_ref, sgb_ref, ust_ref) = refs
    tm = x_ref.shape[0]

    h = _rms_rows(x_ref[...], ng_ref[...]) * (1.0 + sc_ref[...]) + sh_ref[...]
    hb = h.astype(BF16)

    def proj(off, width):
        return jnp.dot(hb, w_ref[:, off:off + width], preferred_element_type=F32)

    u = proj(OFF_UC, CONV_DIM) * proj(OFF_UH, CONV_DIM)
    row = lax.broadcasted_iota(jnp.int32, (tm, 1), 0)
    if seg_len is None:
        @pl.when(pl.program_id(0) == 0)
        def _():
            carry_ref[...] = jnp.zeros_like(carry_ref)
        prev2 = carry_ref[SUBLANES - 2:SUBLANES - 1, :]
        prev1 = carry_ref[SUBLANES - 1:SUBLANES, :]
        pos = row
        p1 = prev1
        p2 = jnp.where(pos == 0, prev2, prev1)
    else:
        pos = row % seg_len
        p1 = p1_ref[...]
        p2 = p2_ref[...]
    u1 = jnp.where(pos == 0, p1, pltpu.roll(u, 1, 0))
    u2 = jnp.where(pos < 2, p2, pltpu.roll(u, 2, 0))
    conv = (cw_ref[0:1, :] * u2 + cw_ref[1:2, :] * u1) + cw_ref[2:3, :] * u
    if seg_len is None:
        carry_ref[...] = u[tm - SUBLANES:tm, :]
        ust_ref[...] = u[tm - SUBLANES:tm, :]
    else:
        ust_ref[...] = u
    branch_a = (proj(OFF_UB, CONV_DIM) * conv) * _silu(proj(OFF_ZA, CONV_DIM))
    ap_ref[...] = jax.nn.sigmoid(proj(OFF_GA, D_MODEL)) * jnp.dot(
        branch_a.astype(BF16), wpa_ref[...], preferred_element_type=F32)

    qf = proj(OFF_Q, ATT_DIM)
    for hd in range(N_HEADS):
        sl = slice(hd * HEAD_DIM, (hd + 1) * HEAD_DIM)
        qb_ref[:, sl] = (_rms_rows(qf[:, sl], qn_ref[...]) * (HEAD_DIM ** -0.5 * LOG2_E)).astype(BF16)
    kf = proj(OFF_K, KV_DIM)
    for g in range(N_KV_HEADS):
        sl = slice(g * HEAD_DIM, (g + 1) * HEAD_DIM)
        kn = _rms_rows(kf[:, sl], kn_ref[...])
        kf_ref[:, sl] = kn
        kb_ref[:, sl] = kn.astype(BF16)
    vf = proj(OFF_V, KV_DIM)
    vf_ref[...] = vf
    vb_ref[...] = vf.astype(BF16)
    qib_ref[...] = proj(OFF_QI, QI_DIM).astype(BF16)
    kw = proj(OFF_KW, LANES)
    kif_ref[...] = kw[:, 0:IDX_DIM]
    kib_ref[...] = kw[:, 0:IDX_DIM].astype(BF16)
    wq_ref[...] = (kw[:, IDX_DIM:IDX_DIM + N_IDX_HEADS] * (N_IDX_HEADS ** -0.5)) * (IDX_DIM ** -0.5)
    zb_ref[...] = _silu(proj(OFF_ZB, ATT_DIM))
    sgb_ref[...] = jax.nn.sigmoid(proj(OFF_GB, D_MODEL))


def _row_tile(t):
    for tm in (512, 256, 128, 64, 32, 16, 8):
        if t % tm == 0:
            return tm
    raise ValueError(f"row count {t} is not a multiple of {SUBLANES}")


def _projection(x, ng, scale, shift, w_in_p, conv_w, qn, kn, w_pa, prev=None, seg_len=None):
    t = x.shape[0]
    tm = _row_tile(t) if seg_len is None else t
    gr = scale.shape[0]
    assert gr in (1, t)
    row = lambda i: (i, 0)
    const = lambda i: (0, 0)
    mod_spec = pl.BlockSpec((1 if gr == 1 else tm, D_MODEL), const if gr == 1 else row)
    in_specs = [
        pl.BlockSpec((tm, D_MODEL), row),
        pl.BlockSpec((1, D_MODEL), const),
        mod_spec, mod_spec,
        pl.BlockSpec((D_MODEL, P_COLS), const),
        pl.BlockSpec((CONV_WIDTH, CONV_DIM), const),
        pl.BlockSpec((1, HEAD_DIM), const),
        pl.BlockSpec((1, HEAD_DIM), const),
        pl.BlockSpec((CONV_DIM, D_MODEL), const),
    ]
    args = [x, ng, scale, shift, w_in_p, conv_w, qn, kn, w_pa]
    scratch = []
    if seg_len is None:
        ust_shape, ust_spec = (SUBLANES, CONV_DIM), pl.BlockSpec((SUBLANES, CONV_DIM), const)
        scratch.append(pltpu.VMEM((SUBLANES, CONV_DIM), F32))
    else:
        in_specs += [pl.BlockSpec((tm, CONV_DIM), row)] * 2
        args += list(prev)
        ust_shape, ust_spec = (t, CONV_DIM), pl.BlockSpec((tm, CONV_DIM), row)
    widths = [(D_MODEL, F32), (ATT_DIM, BF16), (KV_DIM, F32), (KV_DIM, BF16), (KV_DIM, F32),
              (KV_DIM, BF16), (QI_DIM, BF16), (IDX_DIM, F32), (IDX_DIM, BF16), (N_IDX_HEADS, F32),
              (ATT_DIM, F32), (D_MODEL, F32)]
    out_shape = [jax.ShapeDtypeStruct((t, w), dt) for w, dt in widths]
    out_specs = [pl.BlockSpec((tm, w), row) for w, _ in widths]
    out_shape.append(jax.ShapeDtypeStruct(ust_shape, F32))
    out_specs.append(ust_spec)
    return pl.pallas_call(
        functools.partial(_proj_kernel, seg_len=seg_len),
        grid=(t // tm,),
        in_specs=in_specs,
        out_specs=out_specs,
        out_shape=out_shape,
        scratch_shapes=scratch,
        compiler_params=pltpu.CompilerParams(
            dimension_semantics=("arbitrary",), vmem_limit_bytes=V7X_VMEM_LIMIT_BYTES),
        name="proj_prompt" if seg_len is None else "proj_sample",
    )(*args)


def _out_kernel(ap_ref, at_ref, zb_ref, sgb_ref, x_ref, gate_ref, wpb_ref, wout_ref, y_ref):
    branch_b = (at_ref[...] * zb_ref[...]).astype(BF16)
    merged = ap_ref[...] + sgb_ref[...] * jnp.dot(branch_b, wpb_ref[...], preferred_element_type=F32)
    y_ref[...] = x_ref[...] + gate_ref[...] * jnp.dot(
        merged.astype(BF16), wout_ref[...], preferred_element_type=F32)


def _output(a_part, attn, zb, sgb, x, gate, w_pb, w_out):
    t = x.shape[0]
    tm = _row_tile(t)
    gr = gate.shape[0]
    assert gr in (1, t)
    row = lambda i: (i, 0)
    const = lambda i: (0, 0)
    act = pl.BlockSpec((tm, D_MODEL), row)
    return pl.pallas_call(
        _out_kernel,
        grid=(t // tm,),
        in_specs=[act, act, act, act, act,
                  pl.BlockSpec((1 if gr == 1 else tm, D_MODEL), const if gr == 1 else row),
                  pl.BlockSpec((ATT_DIM, D_MODEL), const),
                  pl.BlockSpec((D_MODEL, D_MODEL), const)],
        out_specs=act,
        out_shape=jax.ShapeDtypeStruct((t, D_MODEL), F32),
        compiler_params=pltpu.CompilerParams(
            dimension_semantics=("arbitrary",), vmem_limit_bytes=V7X_VMEM_LIMIT_BYTES),
        name="out_proj",
    )(a_part, attn, zb, sgb, x, gate, w_pb, w_out)


def _key_to_float(key):
    bits = jnp.where(key >= 0, key, key ^ jnp.int32(0x7FFFFFFF))
    return lax.bitcast_convert_type(bits, F32)


def _fold_keys(x, key_axis):
    if key_axis == 0:
        n, r = x.shape
        return jnp.sum(x.reshape(n // SUBLANES, SUBLANES, r), axis=0)
    acc = x[:, 0:LANES]
    for c in range(1, x.shape[1] // LANES):
        acc = acc + x[:, c * LANES:(c + 1) * LANES]
    return acc


def _count(segments, pred, stat_shape, key_axis):
    fold_shape = (SUBLANES, stat_shape[1]) if key_axis == 0 else (stat_shape[0], LANES)
    total = jnp.zeros(fold_shape, F32)
    for ref, n_tiles in segments:
        def body(j, cnt, ref=ref):
            return cnt + _fold_keys(jnp.where(pred(ref[j]), 1.0, 0.0), key_axis)
        total = lax.fori_loop(0, n_tiles, body, total)
    return jnp.sum(total, axis=key_axis, keepdims=True)


def _truncate_to_bf16(x):
    bits = lax.bitcast_convert_type(x, jnp.int32) & jnp.int32(-65536)
    return lax.bitcast_convert_type(bits, F32).astype(BF16)


def _count_coarse(coarse, thr):
    pack = 2 * SUBLANES
    total = jnp.zeros((pack, thr.shape[1]), F32)
    one, zero = jnp.ones((), thr.dtype), jnp.zeros((), thr.dtype)
    for ref, n_tiles in coarse:
        def body(j, cnt, ref=ref):
            hit = jnp.where(ref[j] >= thr, one, zero)
            part = hit[0:pack]
            for c in range(1, hit.shape[0] // pack):
                part = part + hit[c * pack:(c + 1) * pack]
            return cnt + part.astype(F32)
        total = lax.fori_loop(0, n_tiles, body, total)
    return jnp.sum(total, axis=0, keepdims=True)


def _kth_largest(segments, stat_shape, k, key_axis, coarse=None):
    kf = float(k)

    def search(first_bit, last_bit, count_fn, init):
        def cond(carry):
            i, _, cge = carry
            return (i < last_bit) & (jnp.max(jnp.where(cge == kf, 0.0, 1.0)) > 0.5)

        def body(carry):
            i, prefix, cge = carry
            cand = prefix ^ jnp.left_shift(jnp.int32(1), 31 - i)
            cnt = count_fn(_key_to_float(cand))
            take = cnt >= kf
            return i + 1, jnp.where(take, cand, prefix), jnp.where(take, cnt, cge)

        _, prefix, cge = lax.while_loop(cond, body, (jnp.int32(first_bit),) + init)
        return prefix, cge

    state = (jnp.full(stat_shape, INT_MIN, jnp.int32), jnp.zeros(stat_shape, F32))
    first_fine = 0
    if coarse is not None:
        assert key_axis == 0
        first_fine = 16
        state = search(0, first_fine, lambda t: _count_coarse(coarse, _truncate_to_bf16(t)), state)
    prefix, cge = search(first_fine, 32,
                         lambda t: _count(segments, lambda tile: tile >= t, stat_shape, key_axis), state)
    prefix = jnp.maximum(prefix, jnp.int32(KEY_LOWEST_FINITE))
    return _key_to_float(prefix), cge


def _drop_excess_ties(segments, thr, cge, k, key_axis):
    excess = cge > float(k)
    n_eq = _count(segments, lambda tile: tile == thr, thr.shape, key_axis)
    need = float(k) - (cge - n_eq)
    seen = jnp.zeros(thr.shape, F32)
    for ref, n_tiles in segments:
        width = ref.shape[1 + key_axis]
        cw = min(width, 2 * LANES)
        r_i = lax.broadcasted_iota(jnp.int32, (cw, cw), 0)
        c_i = lax.broadcasted_iota(jnp.int32, (cw, cw), 1)
        tri = jnp.where(r_i >= c_i if key_axis == 0 else r_i <= c_i, 1.0, 0.0).astype(BF16)

        def drop_body(j, seen, ref=ref, width=width, cw=cw, tri=tri):
            for c in range(width // cw):
                idx = (j, slice(c * cw, (c + 1) * cw), slice(None)) if key_axis == 0 else (
                    j, slice(None), slice(c * cw, (c + 1) * cw))
                tile = ref[idx]
                eq = tile == thr
                eq_b = jnp.where(eq, 1.0, 0.0).astype(BF16)
                within = (jnp.dot(tri, eq_b, preferred_element_type=F32) if key_axis == 0
                          else jnp.dot(eq_b, tri, preferred_element_type=F32))
                drop = eq & excess & (seen + within > need)
                ref[idx] = jnp.where(drop, -jnp.inf, tile)
                seen = seen + jnp.sum(jnp.where(eq, 1.0, 0.0), axis=key_axis, keepdims=True)
            return seen

        seen = lax.fori_loop(0, n_tiles, drop_body, seen)


def _select_threshold(segments, stat_shape, k, key_axis, coarse=None):
    thr, cge = _kth_largest(segments, stat_shape, k, key_axis, coarse)

    @pl.when(jnp.max(cge) > float(k))
    def _():
        _drop_excess_ties(segments, thr, cge, k, key_axis)

    return thr


def _softmax_step_t(q, kt, vt_aug, sels, m_old, acc_old):
    width = sels[0].shape[1]
    rows = sels[0].shape[0]
    s = lax.dot_general(kt, q, _NT, preferred_element_type=F32)
    s = jnp.concatenate(
        [jnp.concatenate([jnp.where(sel, s[t * rows:(t + 1) * rows, r * width:(r + 1) * width], MASKED_LOGIT)
                          for r in range(q.shape[0] // width)], axis=1)
         for t, sel in enumerate(sels)], axis=0)
    m_new = jnp.maximum(m_old, jnp.max(s, axis=0, keepdims=True))
    p = jnp.exp2(s - m_new)
    alpha = jnp.exp2(m_old - m_new)
    acc_new = alpha * acc_old + jnp.dot(vt_aug, p.astype(BF16), preferred_element_type=F32)
    return m_new, acc_new


def _prompt_attn_kernel(q_ref, qi_ref, wqt_ref, k_ref, vt_ref, ki_ref, o_ref,
                        s_ref, c_ref, m_ref, acc_ref, *, top_k):
    tq = q_ref.shape[0]
    tk = s_ref.shape[1]
    i = pl.program_id(0)

    qi = qi_ref[...]
    wqt = wqt_ref[...]
    qi_all = jnp.concatenate([qi[:, h * IDX_DIM:(h + 1) * IDX_DIM] for h in range(N_IDX_HEADS)], axis=0)
    w_all = jnp.concatenate([wqt[h:h + 1, :] for h in range(N_IDX_HEADS)], axis=1)

    def scores(kt):
        terms = jnp.maximum(lax.dot_general(kt, qi_all, _NT, preferred_element_type=F32), 0.0) * w_all
        acc = terms[:, 0:tq]
        for h in range(1, N_IDX_HEADS):
            acc = acc + terms[:, h * tq:(h + 1) * tq]
        return acc + 0.0

    def key_rows(j):
        return pl.ds(pl.multiple_of(j * tk, tk), tk)

    def put_scores(j, sc):
        s_ref[j] = sc
        c_ref[j] = _truncate_to_bf16(sc)

    def score_body(j, carry):
        put_scores(j, scores(ki_ref[key_rows(j), :]))
        return carry

    lax.fori_loop(0, i, score_body, 0)
    visible = (lax.broadcasted_iota(jnp.int32, (tk, tq), 0)
               <= lax.broadcasted_iota(jnp.int32, (tk, tq), 1))
    put_scores(i, jnp.where(visible, scores(ki_ref[key_rows(i), :]), -jnp.inf))

    thr = _select_threshold([(s_ref, i + 1)], (1, tq), top_k, key_axis=0, coarse=[(c_ref, i + 1)])

    span = vt_ref.shape[3] // tk
    n_steps = (i + span) // span
    for extra in range(1, span):
        @pl.when(i + extra < n_steps * span)
        def _():
            s_ref[i + extra] = jnp.full((tk, tq), -jnp.inf, F32)
    m_ref[...] = jnp.full(m_ref.shape, MASKED_LOGIT, F32)
    acc_ref[...] = jnp.zeros(acc_ref.shape, F32)

    q_groups = [jnp.concatenate([q_ref[:, h * HEAD_DIM:(h + 1) * HEAD_DIM]
                                 for h in range(g * GROUP, (g + 1) * GROUP)], axis=0)
                for g in range(N_KV_HEADS)]

    def attend_body(n, carry):
        sels = [s_ref[n * span + t] >= thr for t in range(span)]
        rows = pl.ds(pl.multiple_of(n * (span * tk), span * tk), span * tk)
        for g in range(N_KV_HEADS):
            m_new, acc_new = _softmax_step_t(
                q_groups[g], k_ref[rows, g * HEAD_DIM:(g + 1) * HEAD_DIM], vt_ref[n, g], sels,
                m_ref[g], acc_ref[g])
            m_ref[g] = m_new
            acc_ref[g] = acc_new
        return carry

    lax.fori_loop(0, n_steps, attend_body, 0)
    for g in range(N_KV_HEADS):
        acc = acc_ref[g]
        out_t = acc[0:HEAD_DIM] * (1.0 / acc[HEAD_DIM:HEAD_DIM + 1])
        for r in range(GROUP):
            h = g * GROUP + r
            o_ref[:, h * HEAD_DIM:(h + 1) * HEAD_DIM] = out_t[:, r * tq:(r + 1) * tq].T


def _prompt_attention(qb, qib, wq, kb, vb, kib):
    t = qb.shape[0]
    tq = min(2 * LANES, t)
    assert t % tq == 0
    nt = t // tq
    span = 4 if nt % 4 == 0 else (2 if nt % 2 == 0 else 1)
    top_k = min(TOPK_MAX, t // 4)
    vt = vb.reshape(nt // span, span * tq, N_KV_HEADS, HEAD_DIM).transpose(0, 2, 3, 1)
    vt = jnp.concatenate([vt, jnp.ones((nt // span, N_KV_HEADS, SUBLANES, span * tq), BF16)], axis=2)
    row = lambda i: (i, 0)
    const = lambda i: (0, 0)
    return pl.pallas_call(
        functools.partial(_prompt_attn_kernel, top_k=top_k),
        grid=(nt,),
        in_specs=[
            pl.BlockSpec((tq, ATT_DIM), row),
            pl.BlockSpec((tq, QI_DIM), row),
            pl.BlockSpec((N_IDX_HEADS, tq), lambda i: (0, i)),
            pl.BlockSpec((t, KV_DIM), const),
            pl.BlockSpec(vt.shape, lambda i: (0, 0, 0, 0)),
            pl.BlockSpec((t, IDX_DIM), const),
        ],
        out_specs=pl.BlockSpec((tq, ATT_DIM), row),
        out_shape=jax.ShapeDtypeStruct((t, ATT_DIM), F32),
        scratch_shapes=[
            pltpu.VMEM((nt, tq, tq), F32),
            pltpu.VMEM((nt, tq, tq), BF16),
            pltpu.VMEM((N_KV_HEADS, 1, GROUP * tq), F32),
            pltpu.VMEM((N_KV_HEADS, HEAD_DIM + SUBLANES, GROUP * tq), F32),
        ],
        compiler_params=pltpu.CompilerParams(
            dimension_semantics=("arbitrary",), vmem_limit_bytes=V7X_VMEM_LIMIT_BYTES),
        name="attn_prompt",
    )(qb, qib, wq.T, kb, vt, kib)


def _softmax_step(q, kt, vt, sel, m_old, l_old, acc_old):
    s = lax.dot_general(q, kt, _NT, preferred_element_type=F32)
    s = jnp.where(sel, s, MASKED_LOGIT)
    m_new = jnp.maximum(m_old, jnp.max(s, axis=-1, keepdims=True))
    p = jnp.exp2(s - m_new)
    alpha = jnp.exp2(m_old - m_new)
    l_new = alpha * l_old + jnp.sum(p, axis=-1, keepdims=True)
    acc_new = alpha * acc_old + jnp.dot(p.astype(BF16), vt, preferred_element_type=F32)
    return m_new, l_new, acc_new


def _sample_attn_kernel(pt_ref, qi_ref, wq_ref, q_ref, kin_ref, kn_ref, vn_ref,
                        ck_hbm, cv_hbm, ckit_hbm, o_ref,
                        kbuf, vbuf, kitbuf, s_main, s_tail, sem, *, layer, top_k, pages_per_chunk):
    b = pl.program_id(0)
    n_chunks, n_tok, chunk = s_main.shape
    page = chunk // pages_per_chunk

    def page_id(c, p):
        return pt_ref[b, c * pages_per_chunk + p]

    def kv_copies(hbm, buf, sem_row, c, slot):
        rows = page * N_KV_HEADS
        return [pltpu.make_async_copy(hbm.at[layer, page_id(c, p)],
                                      buf.at[slot, pl.ds(p * rows, rows)], sem.at[sem_row, slot])
                for p in range(pages_per_chunk)]

    def kit_copies(c, slot):
        return [pltpu.make_async_copy(ckit_hbm.at[layer, page_id(c, p)],
                                      kitbuf.at[slot, :, pl.ds(p * page, page)], sem.at[0, slot])
                for p in range(pages_per_chunk)]

    def start(copies):
        for cp in copies:
            cp.start()

    def wait(copies):
        for cp in copies:
            cp.wait()

    qi = qi_ref[0]
    wq = wq_ref[0]
    w_cols = [wq[:, h:h + 1] for h in range(N_IDX_HEADS)]

    def head_sum(s_all):
        acc = None
        for h in range(N_IDX_HEADS):
            term = jnp.maximum(s_all[h * n_tok:(h + 1) * n_tok, :], 0.0) * w_cols[h]
            acc = term if acc is None else acc + term
        return acc + 0.0

    start(kit_copies(0, 0))

    def score_body(c, carry):
        slot = c % 2

        @pl.when(c + 1 < n_chunks)
        def _():
            start(kit_copies(c + 1, 1 - slot))

        wait(kit_copies(c, slot))
        s_main[c] = head_sum(jnp.dot(qi, kitbuf[slot].astype(BF16), preferred_element_type=F32))
        return carry

    lax.fori_loop(0, n_chunks, score_body, 0)
    col = lax.broadcasted_iota(jnp.int32, (n_tok, LANES), 1)
    tok = lax.broadcasted_iota(jnp.int32, (n_tok, LANES), 0)
    s_new = head_sum(lax.dot_general(qi, kin_ref[0], _NT, preferred_element_type=F32))
    s_tail[0] = jnp.where(col <= tok, s_new, -jnp.inf)

    start(kv_copies(ck_hbm, kbuf, 1, 0, 0))
    start(kv_copies(cv_hbm, vbuf, 2, 0, 0))

    thr = _select_threshold([(s_main, n_chunks), (s_tail, 1)], (n_tok, 1), top_k, key_axis=1)

    rows = GROUP * n_tok
    thr_rows = jnp.concatenate([thr] * GROUP, axis=0)

    def attend(state, kv_of, s_tok):
        sel = jnp.concatenate([s_tok] * GROUP, axis=0) >= thr_rows
        new_state = []
        for g in range(N_KV_HEADS):
            kt, vt = kv_of(g)
            new_state.append(_softmax_step(q_ref[0, g], kt, vt, sel, *state[g]))
        return tuple(new_state)

    def attend_body(c, state):
        slot = c % 2

        @pl.when(c + 1 < n_chunks)
        def _():
            start(kv_copies(ck_hbm, kbuf, 1, c + 1, 1 - slot))
            start(kv_copies(cv_hbm, vbuf, 2, c + 1, 1 - slot))

        wait(kv_copies(ck_hbm, kbuf, 1, c, slot))
        wait(kv_copies(cv_hbm, vbuf, 2, c, slot))

        def kv_of(g):
            head_rows = pl.ds(g, chunk, stride=N_KV_HEADS)
            return kbuf[slot, head_rows, :].astype(BF16), vbuf[slot, head_rows, :].astype(BF16)

        return attend(state, kv_of, s_main[c])

    init = tuple((jnp.full((rows, 1), MASKED_LOGIT, F32), jnp.zeros((rows, 1), F32),
                  jnp.zeros((rows, HEAD_DIM), F32)) for _ in range(N_KV_HEADS))
    state = lax.fori_loop(0, n_chunks, attend_body, init)

    def new_kv(g):
        cols = slice(g * HEAD_DIM, (g + 1) * HEAD_DIM)
        return kn_ref[0, :, cols], vn_ref[0, :, cols]

    state = attend(state, new_kv, s_tail[0])
    for g in range(N_KV_HEADS):
        _, l_fin, acc_fin = state[g]
        o_ref[0, g] = acc_fin * (1.0 / l_fin)


def _sample_attention(qb, qib, wq, kb, vb, kib, cache_k_rows, cache_v_rows, cache_kidx_t, page_table, layer):
    nb, n_pages = page_table.shape
    n_tok = qb.shape[0] // nb
    page = cache_kidx_t.shape[3]
    assert n_tok == SUBLANES and n_tok <= LANES
    pages_per_chunk = 16
    while n_pages % pages_per_chunk:
        pages_per_chunk //= 2
    n_chunks = n_pages // pages_per_chunk
    chunk = pages_per_chunk * page
    top_k = min(TOPK_MAX, (n_pages * page + n_tok) // 4)

    q_s = qb.reshape(nb, n_tok, N_KV_HEADS, GROUP, HEAD_DIM).transpose(0, 2, 3, 1, 4)
    q_s = q_s.reshape(nb, N_KV_HEADS, GROUP * n_tok, HEAD_DIM)
    qi_s = qib.reshape(nb, n_tok, N_IDX_HEADS, IDX_DIM).transpose(0, 2, 1, 3)
    qi_s = qi_s.reshape(nb, N_IDX_HEADS * n_tok, IDX_DIM)
    wq_s = wq.reshape(nb, n_tok, N_IDX_HEADS)
    pad = lambda a: jnp.pad(a.reshape(nb, n_tok, a.shape[-1]), ((0, 0), (0, LANES - n_tok), (0, 0)))
    kin_s, kn_s, vn_s = pad(kib), pad(kb), pad(vb)

    seq3 = lambda b, pt: (b, 0, 0)
    seq4 = lambda b, pt: (b, 0, 0, 0)
    out = pl.pallas_call(
        functools.partial(_sample_attn_kernel, layer=layer, top_k=top_k,
                          pages_per_chunk=pages_per_chunk),
        grid_spec=pltpu.PrefetchScalarGridSpec(
            num_scalar_prefetch=1,
            grid=(nb,),
            in_specs=[
                pl.BlockSpec((1, N_IDX_HEADS * n_tok, IDX_DIM), seq3),
                pl.BlockSpec((1, n_tok, N_IDX_HEADS), seq3),
                pl.BlockSpec((1, N_KV_HEADS, GROUP * n_tok, HEAD_DIM), seq4),
                pl.BlockSpec((1, LANES, IDX_DIM), seq3),
                pl.BlockSpec((1, LANES, KV_DIM), seq3),
                pl.BlockSpec((1, LANES, KV_DIM), seq3),
                pl.BlockSpec(memory_space=pl.ANY),
                pl.BlockSpec(memory_space=pl.ANY),
                pl.BlockSpec(memory_space=pl.ANY),
            ],
            out_specs=pl.BlockSpec((1, N_KV_HEADS, GROUP * n_tok, HEAD_DIM), seq4),
            scratch_shapes=[
                pltpu.VMEM((2, chunk * N_KV_HEADS, HEAD_DIM), F32),
                pltpu.VMEM((2, chunk * N_KV_HEADS, HEAD_DIM), F32),
                pltpu.VMEM((2, IDX_DIM, chunk), F32),
                pltpu.VMEM((n_chunks, n_tok, chunk), F32),
                pltpu.VMEM((1, n_tok, LANES), F32),
                pltpu.SemaphoreType.DMA((3, 2)),
            ],
        ),
        out_shape=jax.ShapeDtypeStruct((nb, N_KV_HEADS, GROUP * n_tok, HEAD_DIM), F32),
        compiler_params=pltpu.CompilerParams(
            dimension_semantics=("arbitrary",), vmem_limit_bytes=V7X_VMEM_LIMIT_BYTES),
        name="attn_sample",
    )(page_table, qi_s, wq_s, q_s, kin_s, kn_s, vn_s, cache_k_rows, cache_v_rows, cache_kidx_t)
    out = out.reshape(nb, N_KV_HEADS, GROUP, n_tok, HEAD_DIM).transpose(0, 3, 1, 2, 4)
    return out.reshape(nb * n_tok, ATT_DIM)


def _pack_w_in(w):
    ends, off = [], 0
    for size in (CONV_DIM, CONV_DIM, CONV_DIM, CONV_DIM, ATT_DIM, KV_DIM, KV_DIM, QI_DIM, IDX_DIM,
                 N_IDX_HEADS, ATT_DIM, D_MODEL, D_MODEL):
        ends.append((off, off + size))
        off += size
    (uh, ub, uc, za, q, k, v, qi, ki, wi, zb, ga, gb) = [w[:, a:b] for a, b in ends]
    pad = jnp.zeros((w.shape[0], LANES - IDX_DIM - N_IDX_HEADS), w.dtype)
    return jnp.concatenate([uh, ub, uc, za, q, k, v, qi, zb, ga, gb, ki, wi, pad], axis=1).astype(BF16)


def kernel(x_prompt, x_sample, c_prompt, c_sample, cache_k, cache_v, cache_kidx, state_conv, page_table,
           norm_g, w_mod, b_mod, w_in, conv_w, q_norm, k_norm, w_proj_a, w_proj_b, w_out):
    depth = w_in.shape[0]
    bp, seq, _ = x_prompt.shape
    nb, n_tok, _ = x_sample.shape
    assert bp == 1
    n_phys, page = cache_k.shape[1], cache_k.shape[2]
    cache_k = cache_k.reshape(depth, n_phys, page * N_KV_HEADS, HEAD_DIM)
    cache_v = cache_v.reshape(depth, n_phys, page * N_KV_HEADS, HEAD_DIM)
    cache_kidx = jnp.swapaxes(cache_kidx, 2, 3)

    mod = _modulation(jnp.concatenate([c_prompt, c_sample], axis=0), w_mod, b_mod)

    yp = x_prompt.reshape(seq, D_MODEL)
    ys = x_sample.reshape(nb * n_tok, D_MODEL)
    outs = [[] for _ in range(8)]
    for l in range(depth):
        w_in_p = _pack_w_in(w_in[l])
        w_pa, w_pb, w_o = (w_proj_a[l].astype(BF16), w_proj_b[l].astype(BF16), w_out[l].astype(BF16))
        ng, qn, kn = norm_g[l][None, :], q_norm[l][None, :], k_norm[l][None, :]

        shift, scale, gate = jnp.split(mod[l, 0:bp], 3, axis=-1)
        (ap, qb, kf, kb, vf, vb, qib, kif, kib, wq, zb, sgb, ust) = _projection(
            yp, ng, scale, shift, w_in_p, conv_w[l], qn, kn, w_pa)
        attn = _prompt_attention(qb, qib, wq, kb, vb, kib)
        yp = _output(ap, attn, zb, sgb, yp, gate, w_pb, w_o)
        outs[0].append(kf.reshape(bp, seq, N_KV_HEADS, HEAD_DIM))
        outs[1].append(vf.reshape(bp, seq, N_KV_HEADS, HEAD_DIM))
        outs[2].append(kif.reshape(bp, seq, IDX_DIM))
        outs[3].append(ust[SUBLANES - (CONV_WIDTH - 1):].reshape(bp, CONV_WIDTH - 1, CONV_DIM))

        shift, scale, gate = [jnp.repeat(m, n_tok, axis=0) for m in jnp.split(mod[l, bp:], 3, axis=-1)]
        st = state_conv[l]
        zeros = jnp.zeros((nb, n_tok - 2, CONV_DIM), F32)
        p1 = jnp.concatenate([st[:, 1:2], st[:, 1:2], zeros], axis=1).reshape(nb * n_tok, CONV_DIM)
        p2 = jnp.concatenate([st[:, 0:1], st[:, 1:2], zeros], axis=1).reshape(nb * n_tok, CONV_DIM)
        (ap, qb, kf, kb, vf, vb, qib, kif, kib, wq, zb, sgb, ust) = _projection(
            ys, ng, scale, shift, w_in_p, conv_w[l], qn, kn, w_pa, prev=(p1, p2), seg_len=n_tok)
        attn = _sample_attention(qb, qib, wq, kb, vb, kib, cache_k, cache_v, cache_kidx, page_table, l)
        ys = _output(ap, attn, zb, sgb, ys, gate, w_pb, w_o)
        outs[4].append(kf.reshape(nb, n_tok, N_KV_HEADS, HEAD_DIM))
        outs[5].append(vf.reshape(nb, n_tok, N_KV_HEADS, HEAD_DIM))
        outs[6].append(kif.reshape(nb, n_tok, IDX_DIM))
        outs[7].append(ust.reshape(nb, n_tok, CONV_DIM)[:, n_tok - (CONV_WIDTH - 1):])

    return (yp.reshape(bp, seq, D_MODEL), ys.reshape(nb, n_tok, D_MODEL),
            *[jnp.stack(o) for o in outs])
```

```python
import functools

import jax
import jax.numpy as jnp
from jax import lax
from jax.experimental import pallas as pl
from jax.experimental.pallas import tpu as pltpu

F32 = jnp.float32
BF16 = jnp.bfloat16

D_MODEL = 1024
CONV_DIM = 1024
CONV_WIDTH = 3
N_HEADS = 8
N_KV_HEADS = 2
HEAD_DIM = 128
ATT_DIM = N_HEADS * HEAD_DIM
KV_DIM = N_KV_HEADS * HEAD_DIM
GROUP = N_HEADS // N_KV_HEADS
N_IDX_HEADS = 8
IDX_DIM = 64
QI_DIM = N_IDX_HEADS * IDX_DIM
TOPK_MAX = 256
EPS = 1e-6
MASKED_LOGIT = -1e30
LOG2_E = 1.4426950408889634
MAX_STATIC_BOUND = 50.0
BOUND_PAD = 1.02

LANES = 128
SUBLANES = 8
V7X_VMEM_LIMIT_BYTES = 56 * 1024 * 1024

OFF_UH = 0
OFF_UB = OFF_UH + CONV_DIM
OFF_UC = OFF_UB + CONV_DIM
OFF_ZA = OFF_UC + CONV_DIM
OFF_Q = OFF_ZA + CONV_DIM
OFF_K = OFF_Q + ATT_DIM
OFF_V = OFF_K + KV_DIM
OFF_QI = OFF_V + KV_DIM
OFF_ZB = OFF_QI + QI_DIM
OFF_GA = OFF_ZB + ATT_DIM
OFF_GB = OFF_GA + D_MODEL
OFF_KW = OFF_GB + D_MODEL
P_COLS = OFF_KW + LANES

KEY_LOWEST_FINITE = -2139095040
INT_MIN = -(2 ** 31)

_NT = (((1,), (1,)), ((), ()))


def _silu(x):
    return x * jax.nn.sigmoid(x)


def _rms_rows(x, g):
    ms = jnp.mean(x * x, axis=-1, keepdims=True)
    return (x * lax.rsqrt(ms + EPS)) * g


def _mod_kernel(c_ref, w_ref, b_ref, o_ref):
    s = _silu(c_ref[...]).astype(BF16)
    o_ref[0] = jnp.dot(s, w_ref[0].astype(BF16), preferred_element_type=F32) + b_ref[0]


def _modulation(c_all, w_mod, b_mod):
    depth = w_mod.shape[0]
    nb = c_all.shape[0]
    return pl.pallas_call(
        _mod_kernel,
        grid=(depth,),
        in_specs=[
            pl.BlockSpec((nb, D_MODEL), lambda l: (0, 0)),
            pl.BlockSpec((1, D_MODEL, 3 * D_MODEL), lambda l: (l, 0, 0)),
            pl.BlockSpec((1, 1, 3 * D_MODEL), lambda l: (l, 0, 0)),
        ],
        out_specs=pl.BlockSpec((1, nb, 3 * D_MODEL), lambda l: (l, 0, 0)),
        out_shape=jax.ShapeDtypeStruct((depth, nb, 3 * D_MODEL), F32),
        compiler_params=pltpu.CompilerParams(
            dimension_semantics=("arbitrary",), vmem_limit_bytes=V7X_VMEM_LIMIT_BYTES),
        name="adaln_mod",
    )(c_all, w_mod, b_mod.reshape(depth, 1, 3 * D_MODEL))


def _proj_kernel(*refs, seg_len):
    if seg_len is None:
        (x_ref, ng_ref, sc_ref, sh_ref, w_ref, cw_ref, qn_ref, kn_ref, wpa_ref,
         ap_ref, qb_ref, kf_ref, kb_ref, vf_ref, vb_ref, qib_ref, kif_ref, kib_ref, wq_ref,
         zb_ref, sgb_ref, ust_ref, kmx_ref, carry_ref) = refs
    else:
        (x_ref, ng_ref, sc_ref, sh_ref, w_ref, cw_ref, qn_ref, kn_ref, wpa_ref, p1_ref, p2_ref,
         ap_ref, qb_ref, kf_ref, kb_ref, vf_ref, vb_ref, qib_ref, kif_ref, kib_ref, wq_ref,
         zb_ref, sgb_ref, ust_ref, kmx_ref) = refs
    tm = x_ref.shape[0]

    h = _rms_rows(x_ref[...], ng_ref[...]) * (1.0 + sc_ref[...]) + sh_ref[...]
    hb = h.astype(BF16)

    def proj(off, width):
        return jnp.dot(hb, w_ref[:, off:off + width], preferred_element_type=F32)

    u = proj(OFF_UC, CONV_DIM) * proj(OFF_UH, CONV_DIM)
    row = lax.broadcasted_iota(jnp.int32, (tm, 1), 0)
    if seg_len is None:
        @pl.when(pl.program_id(0) == 0)
        def _():
            carry_ref[...] = jnp.zeros_like(carry_ref)
        prev2 = carry_ref[SUBLANES - 2:SUBLANES - 1, :]
        prev1 = carry_ref[SUBLANES - 1:SUBLANES, :]
        pos = row
        p1 = prev1
        p2 = jnp.where(pos == 0, prev2, prev1)
    else:
        pos = row % seg_len
        p1 = p1_ref[...]
        p2 = p2_ref[...]
    u1 = jnp.where(pos == 0, p1, pltpu.roll(u, 1, 0))
    u2 = jnp.where(pos < 2, p2, pltpu.roll(u, 2, 0))
    conv = (cw_ref[0:1, :] * u2 + cw_ref[1:2, :] * u1) + cw_ref[2:3, :] * u
    if seg_len is None:
        carry_ref[...] = u[tm - SUBLANES:tm, :]
        ust_ref[...] = u[tm - SUBLANES:tm, :]
    else:
        ust_ref[...] = u
    branch_a = (proj(OFF_UB, CONV_DIM) * conv) * _silu(proj(OFF_ZA, CONV_DIM))
    ap_ref[...] = jax.nn.sigmoid(proj(OFF_GA, D_MODEL)) * jnp.dot(
        branch_a.astype(BF16), wpa_ref[...], preferred_element_type=F32)

    qf = proj(OFF_Q, ATT_DIM)
    for hd in range(N_HEADS):
        sl = slice(hd * HEAD_DIM, (hd + 1) * HEAD_DIM)
        qb_ref[:, sl] = (_rms_rows(qf[:, sl], qn_ref[...]) * (HEAD_DIM ** -0.5 * LOG2_E)).astype(BF16)
    kf = proj(OFF_K, KV_DIM)

    @pl.when(pl.program_id(0) == 0)
    def _():
        kmx_ref[...] = jnp.zeros_like(kmx_ref)

    for g in range(N_KV_HEADS):
        sl = slice(g * HEAD_DIM, (g + 1) * HEAD_DIM)
        kn = _rms_rows(kf[:, sl], kn_ref[...])
        kf_ref[:, sl] = kn
        kb_ref[:, sl] = kn.astype(BF16)
        kr = kn.astype(BF16).astype(F32)
        n2 = jnp.max(jnp.sum(kr * kr, axis=-1, keepdims=True), axis=0, keepdims=True)
        kmx_ref[g:g + 1, :] = jnp.maximum(kmx_ref[g:g + 1, :], n2)
    vf = proj(OFF_V, KV_DIM)
    vf_ref[...] = vf
    vb_ref[...] = vf.astype(BF16)
    qib_ref[...] = proj(OFF_QI, QI_DIM).astype(BF16)
    kw = proj(OFF_KW, LANES)
    kif_ref[...] = kw[:, 0:IDX_DIM]
    kib_ref[...] = kw[:, 0:IDX_DIM].astype(BF16)
    wq_ref[...] = (kw[:, IDX_DIM:IDX_DIM + N_IDX_HEADS] * (N_IDX_HEADS ** -0.5)) * (IDX_DIM ** -0.5)
    zb_ref[...] = _silu(proj(OFF_ZB, ATT_DIM))
    sgb_ref[...] = jax.nn.sigmoid(proj(OFF_GB, D_MODEL))


def _row_tile(t):
    for tm in (512, 256, 128, 64, 32, 16, 8):
        if t % tm == 0:
            return tm
    raise ValueError(f"row count {t} is not a multiple of {SUBLANES}")


def _projection(x, ng, scale, shift, w_in_p, conv_w, qn, kn, w_pa, prev=None, seg_len=None):
    t = x.shape[0]
    tm = _row_tile(t) if seg_len is None else t
    gr = scale.shape[0]
    assert gr in (1, t)
    row = lambda i: (i, 0)
    const = lambda i: (0, 0)
    mod_spec = pl.BlockSpec((1 if gr == 1 else tm, D_MODEL), const if gr == 1 else row)
    in_specs = [
        pl.BlockSpec((tm, D_MODEL), row),
        pl.BlockSpec((1, D_MODEL), const),
        mod_spec, mod_spec,
        pl.BlockSpec((D_MODEL, P_COLS), const),
        pl.BlockSpec((CONV_WIDTH, CONV_DIM), const),
        pl.BlockSpec((1, HEAD_DIM), const),
        pl.BlockSpec((1, HEAD_DIM), const),
        pl.BlockSpec((CONV_DIM, D_MODEL), const),
    ]
    args = [x, ng, scale, shift, w_in_p, conv_w, qn, kn, w_pa]
    scratch = []
    if seg_len is None:
        ust_shape, ust_spec = (SUBLANES, CONV_DIM), pl.BlockSpec((SUBLANES, CONV_DIM), const)
        scratch.append(pltpu.VMEM((SUBLANES, CONV_DIM), F32))
    else:
        in_specs += [pl.BlockSpec((tm, CONV_DIM), row)] * 2
        args += list(prev)
        ust_shape, ust_spec = (t, CONV_DIM), pl.BlockSpec((tm, CONV_DIM), row)
    widths = [(D_MODEL, F32), (ATT_DIM, BF16), (KV_DIM, F32), (KV_DIM, BF16), (KV_DIM, F32),
              (KV_DIM, BF16), (QI_DIM, BF16), (IDX_DIM, F32), (IDX_DIM, BF16), (N_IDX_HEADS, F32),
              (ATT_DIM, F32), (D_MODEL, F32)]
    out_shape = [jax.ShapeDtypeStruct((t, w), dt) for w, dt in widths]
    out_specs = [pl.BlockSpec((tm, w), row) for w, _ in widths]
    out_shape.append(jax.ShapeDtypeStruct(ust_shape, F32))
    out_specs.append(ust_spec)
    out_shape.append(jax.ShapeDtypeStruct((SUBLANES, LANES), F32))
    out_specs.append(pl.BlockSpec((SUBLANES, LANES), const))
    return pl.pallas_call(
        functools.partial(_proj_kernel, seg_len=seg_len),
        grid=(t // tm,),
        in_specs=in_specs,
        out_specs=out_specs,
        out_shape=out_shape,
        scratch_shapes=scratch,
        compiler_params=pltpu.CompilerParams(
            dimension_semantics=("arbitrary",), vmem_limit_bytes=V7X_VMEM_LIMIT_BYTES),
        name="proj_prompt" if seg_len is None else "proj_sample",
    )(*args)


def _out_kernel(ap_ref, at_ref, zb_ref, sgb_ref, x_ref, gate_ref, wpb_ref, wout_ref, y_ref):
    branch_b = (at_ref[...] * zb_ref[...]).astype(BF16)
    merged = ap_ref[...] + sgb_ref[...] * jnp.dot(branch_b, wpb_ref[...], preferred_element_type=F32)
    y_ref[...] = x_ref[...] + gate_ref[...] * jnp.dot(
        merged.astype(BF16), wout_ref[...], preferred_element_type=F32)


def _output(a_part, attn, zb, sgb, x, gate, w_pb, w_out):
    t = x.shape[0]
    tm = _row_tile(t)
    gr = gate.shape[0]
    assert gr in (1, t)
    row = lambda i: (i, 0)
    const = lambda i: (0, 0)
    act = pl.BlockSpec((tm, D_MODEL), row)
    return pl.pallas_call(
        _out_kernel,
        grid=(t // tm,),
        in_specs=[act, act, act, act, act,
                  pl.BlockSpec((1 if gr == 1 else tm, D_MODEL), const if gr == 1 else row),
                  pl.BlockSpec((ATT_DIM, D_MODEL), const),
                  pl.BlockSpec((D_MODEL, D_MODEL), const)],
        out_specs=act,
        out_shape=jax.ShapeDtypeStruct((t, D_MODEL), F32),
        compiler_params=pltpu.CompilerParams(
            dimension_semantics=("arbitrary",), vmem_limit_bytes=V7X_VMEM_LIMIT_BYTES),
        name="out_proj",
    )(a_part, attn, zb, sgb, x, gate, w_pb, w_out)


def _key_to_float(key):
    bits = jnp.where(key >= 0, key, key ^ jnp.int32(0x7FFFFFFF))
    return lax.bitcast_convert_type(bits, F32)


def _tree_sum(parts):
    while len(parts) > 1:
        parts = [parts[i] + parts[i + 1] for i in range(0, len(parts) - 1, 2)] + (
            [parts[-1]] if len(parts) % 2 else [])
    return parts[0]


def _fold_keys(x, key_axis, rows=SUBLANES):
    if key_axis == 0:
        return _tree_sum([x[c * rows:(c + 1) * rows] for c in range(x.shape[0] // rows)])
    return _tree_sum([x[:, c * LANES:(c + 1) * LANES] for c in range(x.shape[1] // LANES)])


def _loop_tiles(n_tiles, body, init):
    if isinstance(n_tiles, int) and n_tiles <= 16:
        for j in range(n_tiles):
            init = body(j, init)
        return init
    return lax.fori_loop(0, n_tiles, body, init)


def _count(segments, pred, stat_shape, key_axis):
    fold_shape = (SUBLANES, stat_shape[1]) if key_axis == 0 else (stat_shape[0], LANES)
    total = jnp.zeros(fold_shape, F32)
    for ref, n_tiles in segments:
        def body(j, cnt, ref=ref):
            return cnt + _fold_keys(jnp.where(pred(ref[j]), 1.0, 0.0), key_axis)
        total = _loop_tiles(n_tiles, body, total)
    return jnp.sum(total, axis=key_axis, keepdims=True)


def _truncate_to_bf16(x):
    bits = lax.bitcast_convert_type(x, jnp.int32) & jnp.int32(-65536)
    return lax.bitcast_convert_type(bits, F32).astype(BF16)


def _count_coarse(coarse, thr):
    pack = 2 * SUBLANES
    total = jnp.zeros((pack, thr.shape[1]), F32)
    one, zero = jnp.ones((), thr.dtype), jnp.zeros((), thr.dtype)
    for ref, n_tiles in coarse:
        def body(j, cnt, ref=ref):
            hit = jnp.where(ref[j] >= thr, one, zero)
            return cnt + _fold_keys(hit, 0, rows=pack).astype(F32)
        total = _loop_tiles(n_tiles, body, total)
    return jnp.sum(total, axis=0, keepdims=True)


def _kth_largest(segments, stat_shape, k, key_axis, coarse=None):
    kf = float(k)

    def step(i, state, count_fn):
        prefix, cge = state
        cand = prefix ^ jnp.left_shift(jnp.int32(1), 31 - i)
        cnt = count_fn(_key_to_float(cand))
        take = cnt >= kf
        return jnp.where(take, cand, prefix), jnp.where(take, cnt, cge)

    def count_fine(t):
        return _count(segments, lambda tile: tile >= t, stat_shape, key_axis)

    state = (jnp.full(stat_shape, INT_MIN, jnp.int32), jnp.zeros(stat_shape, F32))
    first_fine = 0
    if coarse is not None:
        assert key_axis == 0
        first_fine = 16
        state = lax.fori_loop(
            0, first_fine,
            lambda i, st: step(i, st, lambda t: _count_coarse(coarse, _truncate_to_bf16(t))), state)

    def cond(carry):
        i, _, cge = carry
        return (i < 32) & (jnp.max(jnp.where(cge == kf, 0.0, 1.0)) > 0.5)

    def body(carry):
        i, prefix, cge = carry
        return (i + 1,) + step(i, (prefix, cge), count_fine)

    _, prefix, cge = lax.while_loop(cond, body, (jnp.int32(first_fine),) + state)
    prefix = jnp.maximum(prefix, jnp.int32(KEY_LOWEST_FINITE))
    return _key_to_float(prefix), cge


def _drop_excess_ties(segments, thr, cge, k, key_axis):
    excess = cge > float(k)
    n_eq = _count(segments, lambda tile: tile == thr, thr.shape, key_axis)
    need = float(k) - (cge - n_eq)
    seen = jnp.zeros(thr.shape, F32)
    for ref, n_tiles in segments:
        width = ref.shape[1 + key_axis]
        cw = min(width, 2 * LANES)
        r_i = lax.broadcasted_iota(jnp.int32, (cw, cw), 0)
        c_i = lax.broadcasted_iota(jnp.int32, (cw, cw), 1)
        tri = jnp.where(r_i >= c_i if key_axis == 0 else r_i <= c_i, 1.0, 0.0).astype(BF16)

        def drop_body(j, seen, ref=ref, width=width, cw=cw, tri=tri):
            for c in range(width // cw):
                idx = (j, slice(c * cw, (c + 1) * cw), slice(None)) if key_axis == 0 else (
                    j, slice(None), slice(c * cw, (c + 1) * cw))
                tile = ref[idx]
                eq = tile == thr
                eq_b = jnp.where(eq, 1.0, 0.0).astype(BF16)
                within = (jnp.dot(tri, eq_b, preferred_element_type=F32) if key_axis == 0
                          else jnp.dot(eq_b, tri, preferred_element_type=F32))
                drop = eq & excess & (seen + within > need)
                ref[idx] = jnp.where(drop, -jnp.inf, tile)
                seen = seen + jnp.sum(jnp.where(eq, 1.0, 0.0), axis=key_axis, keepdims=True)
            return seen

        seen = lax.fori_loop(0, n_tiles, drop_body, seen)


def _select_threshold(segments, stat_shape, k, key_axis, coarse=None):
    thr, cge = _kth_largest(segments, stat_shape, k, key_axis, coarse)

    @pl.when(jnp.max(cge) > float(k))
    def _():
        _drop_excess_ties(segments, thr, cge, k, key_axis)

    return thr


def _softmax_step_t(q, kt, vt_aug, sels, m_old, acc_old):
    width = sels[0].shape[1]
    rows = sels[0].shape[0]
    s = lax.dot_general(kt, q, _NT, preferred_element_type=F32)
    s = jnp.concatenate(
        [jnp.concatenate([jnp.where(sel, s[t * rows:(t + 1) * rows, r * width:(r + 1) * width], MASKED_LOGIT)
                          for r in range(q.shape[0] // width)], axis=1)
         for t, sel in enumerate(sels)], axis=0)
    m_new = jnp.maximum(m_old, jnp.max(s, axis=0, keepdims=True))
    p = jnp.exp2(s - m_new)
    alpha = jnp.exp2(m_old - m_new)
    acc_new = alpha * acc_old + jnp.dot(vt_aug, p.astype(BF16), preferred_element_type=F32)
    return m_new, acc_new


def _prompt_attn_kernel(q_ref, qi_ref, wqt_ref, k_ref, vt_ref, ki_ref, kmx_ref, o_ref,
                        s_ref, c_ref, m_ref, acc_ref, *, top_k):
    tq = q_ref.shape[0]
    tk = s_ref.shape[1]
    i = pl.program_id(0)

    qi = qi_ref[...]
    wqt = wqt_ref[...]
    qi_all = jnp.concatenate([qi[:, h * IDX_DIM:(h + 1) * IDX_DIM] for h in range(N_IDX_HEADS)], axis=0)
    w_all = jnp.concatenate([wqt[h:h + 1, :] for h in range(N_IDX_HEADS)], axis=1)

    def scores(kt):
        terms = jnp.maximum(lax.dot_general(kt, qi_all, _NT, preferred_element_type=F32), 0.0) * w_all
        acc = terms[:, 0:tq]
        for h in range(1, N_IDX_HEADS):
            acc = acc + terms[:, h * tq:(h + 1) * tq]
        return acc + 0.0

    def key_rows(j):
        return pl.ds(pl.multiple_of(j * tk, tk), tk)

    def put_scores(j, sc):
        s_ref[j] = sc
        c_ref[j] = _truncate_to_bf16(sc)

    span = vt_ref.shape[3] // tk

    def score_span_body(n, carry):
        rows = pl.ds(pl.multiple_of(n * (span * tk), span * tk), span * tk)
        sc = scores(ki_ref[rows, :])
        for t in range(span):
            put_scores(n * span + t, sc[t * tk:(t + 1) * tk])
        return carry

    def score_body(j, carry):
        put_scores(j, scores(ki_ref[key_rows(j), :]))
        return carry

    lax.fori_loop(0, i // span, score_span_body, 0)
    lax.fori_loop((i // span) * span, i, score_body, 0)
    visible = (lax.broadcasted_iota(jnp.int32, (tk, tq), 0)
               <= lax.broadcasted_iota(jnp.int32, (tk, tq), 1))
    put_scores(i, jnp.where(visible, scores(ki_ref[key_rows(i), :]), -jnp.inf))

    thr = _select_threshold([(s_ref, i + 1)], (1, tq), top_k, key_axis=0, coarse=[(c_ref, i + 1)])

    n_steps = (i + span) // span
    for extra in range(1, span):
        @pl.when(i + extra < n_steps * span)
        def _():
            s_ref[i + extra] = jnp.full((tk, tq), -jnp.inf, F32)
    acc_ref[...] = jnp.zeros(acc_ref.shape, F32)

    q_groups = [jnp.concatenate([q_ref[:, h * HEAD_DIM:(h + 1) * HEAD_DIM]
                                 for h in range(g * GROUP, (g + 1) * GROUP)], axis=0)
                for g in range(N_KV_HEADS)]

    ones = jnp.ones((SUBLANES, HEAD_DIM), BF16)

    def sq_norms(h):
        qf = q_ref[:, h * HEAD_DIM:(h + 1) * HEAD_DIM].astype(F32)
        return lax.dot_general(ones, (qf * qf).astype(BF16), _NT, preferred_element_type=F32)[0:1]

    bounds = [jnp.sqrt(jnp.concatenate([sq_norms(h) for h in range(g * GROUP, (g + 1) * GROUP)], axis=1)
                       * kmx_ref[g:g + 1, 0:1]) * BOUND_PAD
              for g in range(N_KV_HEADS)]
    bounded = jnp.max(jnp.maximum(bounds[0], bounds[1])) <= MAX_STATIC_BOUND

    def key_span(n):
        return pl.ds(pl.multiple_of(n * (span * tk), span * tk), span * tk)

    def bounded_body(n, carry):
        masks = [jnp.where(s_ref[n * span + t] >= thr, 1.0, 0.0).astype(BF16) for t in range(span)]
        mask = jnp.concatenate([jnp.concatenate([mk] * GROUP, axis=1) for mk in masks], axis=0)
        for g in range(N_KV_HEADS):
            s = lax.dot_general(k_ref[key_span(n), g * HEAD_DIM:(g + 1) * HEAD_DIM], q_groups[g], _NT,
                                preferred_element_type=F32)
            p = jnp.exp2(s - bounds[g]).astype(BF16) * mask
            acc_ref[g] = acc_ref[g] + jnp.dot(vt_ref[n, g], p, preferred_element_type=F32)
        return carry

    def running_max_body(n, carry):
        sels = [s_ref[n * span + t] >= thr for t in range(span)]
        for g in range(N_KV_HEADS):
            m_new, acc_new = _softmax_step_t(
                q_groups[g], k_ref[key_span(n), g * HEAD_DIM:(g + 1) * HEAD_DIM], vt_ref[n, g], sels,
                m_ref[g], acc_ref[g])
            m_ref[g] = m_new
            acc_ref[g] = acc_new
        return carry

    @pl.when(bounded)
    def _():
        lax.fori_loop(0, n_steps, bounded_body, 0)

    @pl.when(jnp.logical_not(bounded))
    def _():
        m_ref[...] = jnp.full(m_ref.shape, MASKED_LOGIT, F32)
        lax.fori_loop(0, n_steps, running_max_body, 0)

    for g in range(N_KV_HEADS):
        acc = acc_ref[g]
        out_t = acc[0:HEAD_DIM] * (1.0 / acc[HEAD_DIM:HEAD_DIM + 1])
        for r in range(GROUP):
            h = g * GROUP + r
            o_ref[:, h * HEAD_DIM:(h + 1) * HEAD_DIM] = out_t[:, r * tq:(r + 1) * tq].T


def _prompt_attention(qb, qib, wq, kb, vb, kib, kmx):
    t = qb.shape[0]
    tq = min(2 * LANES, t)
    assert t % tq == 0
    nt = t // tq
    span = 4 if nt % 4 == 0 else (2 if nt % 2 == 0 else 1)
    top_k = min(TOPK_MAX, t // 4)
    vt = vb.reshape(nt // span, span * tq, N_KV_HEADS, HEAD_DIM).transpose(0, 2, 3, 1)
    vt = jnp.concatenate([vt, jnp.ones((nt // span, N_KV_HEADS, SUBLANES, span * tq), BF16)], axis=2)
    row = lambda i: (i, 0)
    const = lambda i: (0, 0)
    return pl.pallas_call(
        functools.partial(_prompt_attn_kernel, top_k=top_k),
        grid=(nt,),
        in_specs=[
            pl.BlockSpec((tq, ATT_DIM), row),
            pl.BlockSpec((tq, QI_DIM), row),
            pl.BlockSpec((N_IDX_HEADS, tq), lambda i: (0, i)),
            pl.BlockSpec((t, KV_DIM), const),
            pl.BlockSpec(vt.shape, lambda i: (0, 0, 0, 0)),
            pl.BlockSpec((t, IDX_DIM), const),
            pl.BlockSpec((SUBLANES, LANES), const),
        ],
        out_specs=pl.BlockSpec((tq, ATT_DIM), row),
        out_shape=jax.ShapeDtypeStruct((t, ATT_DIM), F32),
        scratch_shapes=[
            pltpu.VMEM((nt, tq, tq), F32),
            pltpu.VMEM((nt, tq, tq), BF16),
            pltpu.VMEM((N_KV_HEADS, 1, GROUP * tq), F32),
            pltpu.VMEM((N_KV_HEADS, HEAD_DIM + SUBLANES, GROUP * tq), F32),
        ],
        compiler_params=pltpu.CompilerParams(
            dimension_semantics=("arbitrary",), vmem_limit_bytes=V7X_VMEM_LIMIT_BYTES),
        name="attn_prompt",
    )(qb, qib, wq.T, kb, vt, kib, kmx)


def _softmax_step(q, kt, vt, sel, m_old, l_old, acc_old):
    s = lax.dot_general(q, kt, _NT, preferred_element_type=F32)
    s = jnp.where(sel, s, MASKED_LOGIT)
    m_new = jnp.maximum(m_old, jnp.max(s, axis=-1, keepdims=True))
    p = jnp.exp2(s - m_new)
    alpha = jnp.exp2(m_old - m_new)
    l_new = alpha * l_old + jnp.sum(p, axis=-1, keepdims=True)
    acc_new = alpha * acc_old + jnp.dot(p.astype(BF16), vt, preferred_element_type=F32)
    return m_new, l_new, acc_new


def _sample_attn_kernel(pt_ref, qi_ref, wq_ref, q_ref, kin_ref, kn_ref, vn_ref,
                        ck_hbm, cv_hbm, ckit_hbm, o_ref,
                        kbuf, vbuf, kitbuf, s_main, s_tail, sem, *, layer, top_k, pages_per_chunk):
    b = pl.program_id(0)
    n_chunks, n_tok, chunk = s_main.shape
    page = chunk // pages_per_chunk

    def page_id(c, p):
        return pt_ref[b, c * pages_per_chunk + p]

    def kv_copies(hbm, buf, sem_row, c, slot):
        rows = page * N_KV_HEADS
        return [pltpu.make_async_copy(hbm.at[layer, page_id(c, p)],
                                      buf.at[slot, pl.ds(p * rows, rows)], sem.at[sem_row, slot])
                for p in range(pages_per_chunk)]

    def kit_copies(c, slot):
        return [pltpu.make_async_copy(ckit_hbm.at[layer, page_id(c, p)],
                                      kitbuf.at[slot, :, pl.ds(p * page, page)], sem.at[0, slot])
                for p in range(pages_per_chunk)]

    def start(copies):
        for cp in copies:
            cp.start()

    def wait(copies):
        for cp in copies:
            cp.wait()

    qi = qi_ref[0]
    wq = wq_ref[0]
    w_cols = [wq[:, h:h + 1] for h in range(N_IDX_HEADS)]

    def head_sum(s_all):
        acc = None
        for h in range(N_IDX_HEADS):
            term = jnp.maximum(s_all[h * n_tok:(h + 1) * n_tok, :], 0.0) * w_cols[h]
            acc = term if acc is None else acc + term
        return acc + 0.0

    start(kit_copies(0, 0))

    def score_body(c, carry):
        slot = c % 2

        @pl.when(c + 1 < n_chunks)
        def _():
            start(kit_copies(c + 1, 1 - slot))

        wait(kit_copies(c, slot))
        s_main[c] = head_sum(jnp.dot(qi, kitbuf[slot].astype(BF16), preferred_element_type=F32))
        return carry

    lax.fori_loop(0, n_chunks, score_body, 0)
    col = lax.broadcasted_iota(jnp.int32, (n_tok, LANES), 1)
    tok = lax.broadcasted_iota(jnp.int32, (n_tok, LANES), 0)
    s_new = head_sum(lax.dot_general(qi, kin_ref[0], _NT, preferred_element_type=F32))
    s_tail[0] = jnp.where(col <= tok, s_new, -jnp.inf)

    start(kv_copies(ck_hbm, kbuf, 1, 0, 0))
    start(kv_copies(cv_hbm, vbuf, 2, 0, 0))

    thr = _select_threshold([(s_main, n_chunks), (s_tail, 1)], (n_tok, 1), top_k, key_axis=1)

    rows = GROUP * n_tok
    thr_rows = jnp.concatenate([thr] * GROUP, axis=0)

    def attend(state, kv_of, s_tok):
        sel = jnp.concatenate([s_tok] * GROUP, axis=0) >= thr_rows
        new_state = []
        for g in range(N_KV_HEADS):
            kt, vt = kv_of(g)
            new_state.append(_softmax_step(q_ref[0, g], kt, vt, sel, *state[g]))
        return tuple(new_state)

    def attend_body(c, state):
        slot = c % 2

        @pl.when(c + 1 < n_chunks)
        def _():
            start(kv_copies(ck_hbm, kbuf, 1, c + 1, 1 - slot))
            start(kv_copies(cv_hbm, vbuf, 2, c + 1, 1 - slot))

        wait(kv_copies(ck_hbm, kbuf, 1, c, slot))
        wait(kv_copies(cv_hbm, vbuf, 2, c, slot))

        def kv_of(g):
            head_rows = pl.ds(g, chunk, stride=N_KV_HEADS)
            return kbuf[slot, head_rows, :].astype(BF16), vbuf[slot, head_rows, :].astype(BF16)

        return attend(state, kv_of, s_main[c])

    init = tuple((jnp.full((rows, 1), MASKED_LOGIT, F32), jnp.zeros((rows, 1), F32),
                  jnp.zeros((rows, HEAD_DIM), F32)) for _ in range(N_KV_HEADS))
    state = lax.fori_loop(0, n_chunks, attend_body, init)

    def new_kv(g):
        cols = slice(g * HEAD_DIM, (g + 1) * HEAD_DIM)
        return kn_ref[0, :, cols], vn_ref[0, :, cols]

    state = attend(state, new_kv, s_tail[0])
    for g in range(N_KV_HEADS):
        _, l_fin, acc_fin = state[g]
        o_ref[0, g] = acc_fin * (1.0 / l_fin)


def _sample_attention(qb, qib, wq, kb, vb, kib, cache_k_rows, cache_v_rows, cache_kidx_t, page_table, layer):
    nb, n_pages = page_table.shape
    n_tok = qb.shape[0] // nb
    page = cache_kidx_t.shape[3]
    assert n_tok == SUBLANES and n_tok <= LANES
    pages_per_chunk = 16
    while n_pages % pages_per_chunk:
        pages_per_chunk //= 2
    n_chunks = n_pages // pages_per_chunk
    chunk = pages_per_chunk * page
    top_k = min(TOPK_MAX, (n_pages * page + n_tok) // 4)

    q_s = qb.reshape(nb, n_tok, N_KV_HEADS, GROUP, HEAD_DIM).transpose(0, 2, 3, 1, 4)
    q_s = q_s.reshape(nb, N_KV_HEADS, GROUP * n_tok, HEAD_DIM)
    qi_s = qib.reshape(nb, n_tok, N_IDX_HEADS, IDX_DIM).transpose(0, 2, 1, 3)
    qi_s = qi_s.reshape(nb, N_IDX_HEADS * n_tok, IDX_DIM)
    wq_s = wq.reshape(nb, n_tok, N_IDX_HEADS)
    pad = lambda a: jnp.pad(a.reshape(nb, n_tok, a.shape[-1]), ((0, 0), (0, LANES - n_tok), (0, 0)))
    kin_s, kn_s, vn_s = pad(kib), pad(kb), pad(vb)

    seq3 = lambda b, pt: (b, 0, 0)
    seq4 = lambda b, pt: (b, 0, 0, 0)
    out = pl.pallas_call(
        functools.partial(_sample_attn_kernel, layer=layer, top_k=top_k,
                          pages_per_chunk=pages_per_chunk),
        grid_spec=pltpu.PrefetchScalarGridSpec(
            num_scalar_prefetch=1,
            grid=(nb,),
            in_specs=[
                pl.BlockSpec((1, N_IDX_HEADS * n_tok, IDX_DIM), seq3),
                pl.BlockSpec((1, n_tok, N_IDX_HEADS), seq3),
                pl.BlockSpec((1, N_KV_HEADS, GROUP * n_tok, HEAD_DIM), seq4),
                pl.BlockSpec((1, LANES, IDX_DIM), seq3),
                pl.BlockSpec((1, LANES, KV_DIM), seq3),
                pl.BlockSpec((1, LANES, KV_DIM), seq3),
                pl.BlockSpec(memory_space=pl.ANY),
                pl.BlockSpec(memory_space=pl.ANY),
                pl.BlockSpec(memory_space=pl.ANY),
            ],
            out_specs=pl.BlockSpec((1, N_KV_HEADS, GROUP * n_tok, HEAD_DIM), seq4),
            scratch_shapes=[
                pltpu.VMEM((2, chunk * N_KV_HEADS, HEAD_DIM), F32),
                pltpu.VMEM((2, chunk * N_KV_HEADS, HEAD_DIM), F32),
                pltpu.VMEM((2, IDX_DIM, chunk), F32),
                pltpu.VMEM((n_chunks, n_tok, chunk), F32),
                pltpu.VMEM((1, n_tok, LANES), F32),
                pltpu.SemaphoreType.DMA((3, 2)),
            ],
        ),
        out_shape=jax.ShapeDtypeStruct((nb, N_KV_HEADS, GROUP * n_tok, HEAD_DIM), F32),
        compiler_params=pltpu.CompilerParams(
            dimension_semantics=("arbitrary",), vmem_limit_bytes=V7X_VMEM_LIMIT_BYTES),
        name="attn_sample",
    )(page_table, qi_s, wq_s, q_s, kin_s, kn_s, vn_s, cache_k_rows, cache_v_rows, cache_kidx_t)
    out = out.reshape(nb, N_KV_HEADS, GROUP, n_tok, HEAD_DIM).transpose(0, 3, 1, 2, 4)
    return out.reshape(nb * n_tok, ATT_DIM)


def _pack_w_in(w):
    ends, off = [], 0
    for size in (CONV_DIM, CONV_DIM, CONV_DIM, CONV_DIM, ATT_DIM, KV_DIM, KV_DIM, QI_DIM, IDX_DIM,
                 N_IDX_HEADS, ATT_DIM, D_MODEL, D_MODEL):
        ends.append((off, off + size))
        off += size
    (uh, ub, uc, za, q, k, v, qi, ki, wi, zb, ga, gb) = [w[:, a:b] for a, b in ends]
    pad = jnp.zeros((w.shape[0], LANES - IDX_DIM - N_IDX_HEADS), w.dtype)
    return jnp.concatenate([uh, ub, uc, za, q, k, v, qi, zb, ga, gb, ki, wi, pad], axis=1).astype(BF16)


def kernel(x_prompt, x_sample, c_prompt, c_sample, cache_k, cache_v, cache_kidx, state_conv, page_table,
           norm_g, w_mod, b_mod, w_in, conv_w, q_norm, k_norm, w_proj_a, w_proj_b, w_out):
    depth = w_in.shape[0]
    bp, seq, _ = x_prompt.shape
    nb, n_tok, _ = x_sample.shape
    assert bp == 1
    n_phys, page = cache_k.shape[1], cache_k.shape[2]
    cache_k = cache_k.reshape(depth, n_phys, page * N_KV_HEADS, HEAD_DIM)
    cache_v = cache_v.reshape(depth, n_phys, page * N_KV_HEADS, HEAD_DIM)
    cache_kidx = jnp.swapaxes(cache_kidx, 2, 3)

    mod = _modulation(jnp.concatenate([c_prompt, c_sample], axis=0), w_mod, b_mod)

    yp = x_prompt.reshape(seq, D_MODEL)
    ys = x_sample.reshape(nb * n_tok, D_MODEL)
    outs = [[] for _ in range(8)]
    for l in range(depth):
        w_in_p = _pack_w_in(w_in[l])
        w_pa, w_pb, w_o = (w_proj_a[l].astype(BF16), w_proj_b[l].astype(BF16), w_out[l].astype(BF16))
        ng, qn, kn = norm_g[l][None, :], q_norm[l][None, :], k_norm[l][None, :]

        shift, scale, gate = jnp.split(mod[l, 0:bp], 3, axis=-1)
        (ap, qb, kf, kb, vf, vb, qib, kif, kib, wq, zb, sgb, ust, kmx) = _projection(
            yp, ng, scale, shift, w_in_p, conv_w[l], qn, kn, w_pa)
        attn = _prompt_attention(qb, qib, wq, kb, vb, kib, kmx)
        yp = _output(ap, attn, zb, sgb, yp, gate, w_pb, w_o)
        outs[0].append(kf.reshape(bp, seq, N_KV_HEADS, HEAD_DIM))
        outs[1].append(vf.reshape(bp, seq, N_KV_HEADS, HEAD_DIM))
        outs[2].append(kif.reshape(bp, seq, IDX_DIM))
        outs[3].append(ust[SUBLANES - (CONV_WIDTH - 1):].reshape(bp, CONV_WIDTH - 1, CONV_DIM))

        shift, scale, gate = [jnp.repeat(m, n_tok, axis=0) for m in jnp.split(mod[l, bp:], 3, axis=-1)]
        st = state_conv[l]
        zeros = jnp.zeros((nb, n_tok - 2, CONV_DIM), F32)
        p1 = jnp.concatenate([st[:, 1:2], st[:, 1:2], zeros], axis=1).reshape(nb * n_tok, CONV_DIM)
        p2 = jnp.concatenate([st[:, 0:1], st[:, 1:2], zeros], axis=1).reshape(nb * n_tok, CONV_DIM)
        (ap, qb, kf, kb, vf, vb, qib, kif, kib, wq, zb, sgb, ust, _) = _projection(
            ys, ng, scale, shift, w_in_p, conv_w[l], qn, kn, w_pa, prev=(p1, p2), seg_len=n_tok)
        attn = _sample_attention(qb, qib, wq, kb, vb, kib, cache_k, cache_v, cache_kidx, page_table, l)
        ys = _output(ap, attn, zb, sgb, ys, gate, w_pb, w_o)
        outs[4].append(kf.reshape(nb, n_tok, N_KV_HEADS, HEAD_DIM))
        outs[5].append(vf.reshape(nb, n_tok, N_KV_HEADS, HEAD_DIM))
        outs[6].append(kif.reshape(nb, n_tok, IDX_DIM))
        outs[7].append(ust.reshape(nb, n_tok, CONV_DIM)[:, n_tok - (CONV_WIDTH - 1):])

    return (yp.reshape(bp, seq, D_MODEL), ys.reshape(nb, n_tok, D_MODEL),
            *[jnp.stack(o) for o in outs])
```

```python
import functools

import jax
import jax.numpy as jnp
from jax import lax
from jax.experimental import pallas as pl
from jax.experimental.pallas import tpu as pltpu

F32 = jnp.float32
BF16 = jnp.bfloat16

D_MODEL = 1024
CONV_DIM = 1024
CONV_WIDTH = 3
N_HEADS = 8
N_KV_HEADS = 2
HEAD_DIM = 128
ATT_DIM = N_HEADS * HEAD_DIM
KV_DIM = N_KV_HEADS * HEAD_DIM
GROUP = N_HEADS // N_KV_HEADS
N_IDX_HEADS = 8
IDX_DIM = 64
QI_DIM = N_IDX_HEADS * IDX_DIM
TOPK_MAX = 256
EPS = 1e-6
MASKED_LOGIT = -1e30
LOG2_E = 1.4426950408889634
MAX_STATIC_BOUND = 50.0
BOUND_PAD = 1.02

LANES = 128
SUBLANES = 8
V7X_VMEM_LIMIT_BYTES = 56 * 1024 * 1024

OFF_UH = 0
OFF_UB = OFF_UH + CONV_DIM
OFF_UC = OFF_UB + CONV_DIM
OFF_ZA = OFF_UC + CONV_DIM
OFF_Q = OFF_ZA + CONV_DIM
OFF_K = OFF_Q + ATT_DIM
OFF_V = OFF_K + KV_DIM
OFF_QI = OFF_V + KV_DIM
OFF_ZB = OFF_QI + QI_DIM
OFF_GA = OFF_ZB + ATT_DIM
OFF_GB = OFF_GA + D_MODEL
OFF_KW = OFF_GB + D_MODEL
P_COLS = OFF_KW + LANES

KEY_LOWEST_FINITE = -2139095040
INT_MIN = -(2 ** 31)

_NT = (((1,), (1,)), ((), ()))


def _silu(x):
    return x * jax.nn.sigmoid(x)


def _rms_rows(x, g):
    ms = jnp.mean(x * x, axis=-1, keepdims=True)
    return (x * lax.rsqrt(ms + EPS)) * g


def _mod_kernel(c_ref, w_ref, b_ref, o_ref):
    s = _silu(c_ref[...]).astype(BF16)
    o_ref[0] = jnp.dot(s, w_ref[0].astype(BF16), preferred_element_type=F32) + b_ref[0]


def _modulation(c_all, w_mod, b_mod):
    depth = w_mod.shape[0]
    nb = c_all.shape[0]
    return pl.pallas_call(
        _mod_kernel,
        grid=(depth,),
        in_specs=[
            pl.BlockSpec((nb, D_MODEL), lambda l: (0, 0)),
            pl.BlockSpec((1, D_MODEL, 3 * D_MODEL), lambda l: (l, 0, 0)),
            pl.BlockSpec((1, 1, 3 * D_MODEL), lambda l: (l, 0, 0)),
        ],
        out_specs=pl.BlockSpec((1, nb, 3 * D_MODEL), lambda l: (l, 0, 0)),
        out_shape=jax.ShapeDtypeStruct((depth, nb, 3 * D_MODEL), F32),
        compiler_params=pltpu.CompilerParams(
            dimension_semantics=("arbitrary",), vmem_limit_bytes=V7X_VMEM_LIMIT_BYTES),
        name="adaln_mod",
    )(c_all, w_mod, b_mod.reshape(depth, 1, 3 * D_MODEL))


def _proj_kernel(*refs, seg_len):
    if seg_len is None:
        (x_ref, ng_ref, sc_ref, sh_ref, w_ref, cw_ref, qn_ref, kn_ref, wpa_ref,
         ap_ref, qb_ref, kf_ref, kb_ref, vf_ref, vb_ref, qib_ref, kif_ref, kib_ref, wq_ref,
         zb_ref, sgb_ref, ust_ref, kmx_ref, carry_ref) = refs
    else:
        (x_ref, ng_ref, sc_ref, sh_ref, w_ref, cw_ref, qn_ref, kn_ref, wpa_ref, p1_ref, p2_ref,
         ap_ref, qb_ref, kf_ref, kb_ref, vf_ref, vb_ref, qib_ref, kif_ref, kib_ref, wq_ref,
         zb_ref, sgb_ref, ust_ref, kmx_ref) = refs
    tm = x_ref.shape[0]

    h = _rms_rows(x_ref[...], ng_ref[...]) * (1.0 + sc_ref[...]) + sh_ref[...]
    hb = h.astype(BF16)

    def proj(off, width):
        return jnp.dot(hb, w_ref[:, off:off + width], preferred_element_type=F32)

    u = proj(OFF_UC, CONV_DIM) * proj(OFF_UH, CONV_DIM)
    row = lax.broadcasted_iota(jnp.int32, (tm, 1), 0)
    if seg_len is None:
        @pl.when(pl.program_id(0) == 0)
        def _():
            carry_ref[...] = jnp.zeros_like(carry_ref)
        prev2 = carry_ref[SUBLANES - 2:SUBLANES - 1, :]
        prev1 = carry_ref[SUBLANES - 1:SUBLANES, :]
        pos = row
        p1 = prev1
        p2 = jnp.where(pos == 0, prev2, prev1)
    else:
        pos = row % seg_len
        p1 = p1_ref[...]
        p2 = p2_ref[...]
    u1 = jnp.where(pos == 0, p1, pltpu.roll(u, 1, 0))
    u2 = jnp.where(pos < 2, p2, pltpu.roll(u, 2, 0))
    conv = (cw_ref[0:1, :] * u2 + cw_ref[1:2, :] * u1) + cw_ref[2:3, :] * u
    if seg_len is None:
        carry_ref[...] = u[tm - SUBLANES:tm, :]
        ust_ref[...] = u[tm - SUBLANES:tm, :]
    else:
        ust_ref[...] = u
    branch_a = (proj(OFF_UB, CONV_DIM) * conv) * _silu(proj(OFF_ZA, CONV_DIM))
    ap_ref[...] = jax.nn.sigmoid(proj(OFF_GA, D_MODEL)) * jnp.dot(
        branch_a.astype(BF16), wpa_ref[...], preferred_element_type=F32)

    qf = proj(OFF_Q, ATT_DIM)
    for hd in range(N_HEADS):
        sl = slice(hd * HEAD_DIM, (hd + 1) * HEAD_DIM)
        qb_ref[:, sl] = (_rms_rows(qf[:, sl], qn_ref[...]) * (HEAD_DIM ** -0.5 * LOG2_E)).astype(BF16)
    kf = proj(OFF_K, KV_DIM)

    @pl.when(pl.program_id(0) == 0)
    def _():
        kmx_ref[...] = jnp.zeros_like(kmx_ref)

    for g in range(N_KV_HEADS):
        sl = slice(g * HEAD_DIM, (g + 1) * HEAD_DIM)
        kn = _rms_rows(kf[:, sl], kn_ref[...])
        kf_ref[:, sl] = kn
        kb_ref[:, sl] = kn.astype(BF16)
        kr = kn.astype(BF16).astype(F32)
        n2 = jnp.max(jnp.sum(kr * kr, axis=-1, keepdims=True), axis=0, keepdims=True)
        kmx_ref[g:g + 1, :] = jnp.maximum(kmx_ref[g:g + 1, :], n2)
    vf = proj(OFF_V, KV_DIM)
    vf_ref[...] = vf
    vb_ref[...] = vf.astype(BF16)
    qib_ref[...] = proj(OFF_QI, QI_DIM).astype(BF16)
    kw = proj(OFF_KW, LANES)
    kif_ref[...] = kw[:, 0:IDX_DIM]
    kib_ref[...] = kw[:, 0:IDX_DIM].astype(BF16)
    wq_ref[...] = (kw[:, IDX_DIM:IDX_DIM + N_IDX_HEADS] * (N_IDX_HEADS ** -0.5)) * (IDX_DIM ** -0.5)
    zb_ref[...] = _silu(proj(OFF_ZB, ATT_DIM))
    sgb_ref[...] = jax.nn.sigmoid(proj(OFF_GB, D_MODEL))


def _row_tile(t):
    for tm in (512, 256, 128, 64, 32, 16, 8):
        if t % tm == 0:
            return tm
    raise ValueError(f"row count {t} is not a multiple of {SUBLANES}")


def _projection(x, ng, scale, shift, w_in_p, conv_w, qn, kn, w_pa, prev=None, seg_len=None):
    t = x.shape[0]
    tm = _row_tile(t) if seg_len is None else t
    gr = scale.shape[0]
    assert gr in (1, t)
    row = lambda i: (i, 0)
    const = lambda i: (0, 0)
    mod_spec = pl.BlockSpec((1 if gr == 1 else tm, D_MODEL), const if gr == 1 else row)
    in_specs = [
        pl.BlockSpec((tm, D_MODEL), row),
        pl.BlockSpec((1, D_MODEL), const),
        mod_spec, mod_spec,
        pl.BlockSpec((D_MODEL, P_COLS), const),
        pl.BlockSpec((CONV_WIDTH, CONV_DIM), const),
        pl.BlockSpec((1, HEAD_DIM), const),
        pl.BlockSpec((1, HEAD_DIM), const),
        pl.BlockSpec((CONV_DIM, D_MODEL), const),
    ]
    args = [x, ng, scale, shift, w_in_p, conv_w, qn, kn, w_pa]
    scratch = []
    if seg_len is None:
        ust_shape, ust_spec = (SUBLANES, CONV_DIM), pl.BlockSpec((SUBLANES, CONV_DIM), const)
        scratch.append(pltpu.VMEM((SUBLANES, CONV_DIM), F32))
    else:
        in_specs += [pl.BlockSpec((tm, CONV_DIM), row)] * 2
        args += list(prev)
        ust_shape, ust_spec = (t, CONV_DIM), pl.BlockSpec((tm, CONV_DIM), row)
    widths = [(D_MODEL, F32), (ATT_DIM, BF16), (KV_DIM, F32), (KV_DIM, BF16), (KV_DIM, F32),
              (KV_DIM, BF16), (QI_DIM, BF16), (IDX_DIM, F32), (IDX_DIM, BF16), (N_IDX_HEADS, F32),
              (ATT_DIM, F32), (D_MODEL, F32)]
    out_shape = [jax.ShapeDtypeStruct((t, w), dt) for w, dt in widths]
    out_specs = [pl.BlockSpec((tm, w), row) for w, _ in widths]
    out_shape.append(jax.ShapeDtypeStruct(ust_shape, F32))
    out_specs.append(ust_spec)
    out_shape.append(jax.ShapeDtypeStruct((SUBLANES, LANES), F32))
    out_specs.append(pl.BlockSpec((SUBLANES, LANES), const))
    return pl.pallas_call(
        functools.partial(_proj_kernel, seg_len=seg_len),
        grid=(t // tm,),
        in_specs=in_specs,
        out_specs=out_specs,
        out_shape=out_shape,
        scratch_shapes=scratch,
        compiler_params=pltpu.CompilerParams(
            dimension_semantics=("arbitrary",), vmem_limit_bytes=V7X_VMEM_LIMIT_BYTES),
        name="proj_prompt" if seg_len is None else "proj_sample",
    )(*args)


def _out_kernel(ap_ref, at_ref, zb_ref, sgb_ref, x_ref, gate_ref, wpb_ref, wout_ref, y_ref):
    branch_b = (at_ref[...] * zb_ref[...]).astype(BF16)
    merged = ap_ref[...] + sgb_ref[...] * jnp.dot(branch_b, wpb_ref[...], preferred_element_type=F32)
    y_ref[...] = x_ref[...] + gate_ref[...] * jnp.dot(
        merged.astype(BF16), wout_ref[...], preferred_element_type=F32)


def _output(a_part, attn, zb, sgb, x, gate, w_pb, w_out):
    t = x.shape[0]
    tm = _row_tile(t)
    gr = gate.shape[0]
    assert gr in (1, t)
    row = lambda i: (i, 0)
    const = lambda i: (0, 0)
    act = pl.BlockSpec((tm, D_MODEL), row)
    return pl.pallas_call(
        _out_kernel,
        grid=(t // tm,),
        in_specs=[act, act, act, act, act,
                  pl.BlockSpec((1 if gr == 1 else tm, D_MODEL), const if gr == 1 else row),
                  pl.BlockSpec((ATT_DIM, D_MODEL), const),
                  pl.BlockSpec((D_MODEL, D_MODEL), const)],
        out_specs=act,
        out_shape=jax.ShapeDtypeStruct((t, D_MODEL), F32),
        compiler_params=pltpu.CompilerParams(
            dimension_semantics=("arbitrary",), vmem_limit_bytes=V7X_VMEM_LIMIT_BYTES),
        name="out_proj",
    )(a_part, attn, zb, sgb, x, gate, w_pb, w_out)


def _key_to_float(key):
    bits = jnp.where(key >= 0, key, key ^ jnp.int32(0x7FFFFFFF))
    return lax.bitcast_convert_type(bits, F32)


def _tree_sum(parts):
    while len(parts) > 1:
        parts = [parts[i] + parts[i + 1] for i in range(0, len(parts) - 1, 2)] + (
            [parts[-1]] if len(parts) % 2 else [])
    return parts[0]


def _fold_keys(x, key_axis, rows=SUBLANES):
    if key_axis == 0:
        return _tree_sum([x[c * rows:(c + 1) * rows] for c in range(x.shape[0] // rows)])
    return _tree_sum([x[:, c * LANES:(c + 1) * LANES] for c in range(x.shape[1] // LANES)])


COUNT_UNROLL = 4


def _loop_tiles(n_tiles, body, init):
    if isinstance(n_tiles, int) and n_tiles <= 16:
        for j in range(n_tiles):
            init = body(j, init)
        return init

    def group_body(n, carry):
        for t in range(COUNT_UNROLL):
            carry = body(n * COUNT_UNROLL + t, carry)
        return carry

    n_groups = n_tiles // COUNT_UNROLL
    init = lax.fori_loop(0, n_groups, group_body, init)
    return lax.fori_loop(n_groups * COUNT_UNROLL, n_tiles, body, init)


def _count(segments, pred, stat_shape, key_axis):
    fold_shape = (SUBLANES, stat_shape[1]) if key_axis == 0 else (stat_shape[0], LANES)
    total = jnp.zeros(fold_shape, F32)
    for ref, n_tiles in segments:
        def body(j, cnt, ref=ref):
            return cnt + _fold_keys(jnp.where(pred(ref[j]), 1.0, 0.0), key_axis)
        total = _loop_tiles(n_tiles, body, total)
    return jnp.sum(total, axis=key_axis, keepdims=True)


def _truncate_to_bf16(x):
    bits = lax.bitcast_convert_type(x, jnp.int32) & jnp.int32(-65536)
    return lax.bitcast_convert_type(bits, F32).astype(BF16)


def _count_coarse(coarse, thr):
    pack = 2 * SUBLANES
    total = jnp.zeros((pack, thr.shape[1]), F32)
    one, zero = jnp.ones((), thr.dtype), jnp.zeros((), thr.dtype)
    for ref, n_tiles in coarse:
        def body(j, cnt, ref=ref):
            hit = jnp.where(ref[j] >= thr, one, zero)
            return cnt + _fold_keys(hit, 0, rows=pack).astype(F32)
        total = _loop_tiles(n_tiles, body, total)
    return jnp.sum(total, axis=0, keepdims=True)


def _kth_largest(segments, stat_shape, k, key_axis, coarse=None):
    kf = float(k)

    def step(i, state, count_fn):
        prefix, cge = state
        cand = prefix ^ jnp.left_shift(jnp.int32(1), 31 - i)
        cnt = count_fn(_key_to_float(cand))
        take = cnt >= kf
        return jnp.where(take, cand, prefix), jnp.where(take, cnt, cge)

    def count_fine(t):
        return _count(segments, lambda tile: tile >= t, stat_shape, key_axis)

    state = (jnp.full(stat_shape, INT_MIN, jnp.int32), jnp.zeros(stat_shape, F32))
    first_fine = 0
    if coarse is not None:
        assert key_axis == 0
        first_fine = 16
        state = lax.fori_loop(
            0, first_fine,
            lambda i, st: step(i, st, lambda t: _count_coarse(coarse, _truncate_to_bf16(t))), state)

    def cond(carry):
        i, _, cge = carry
        return (i < 32) & (jnp.max(jnp.where(cge == kf, 0.0, 1.0)) > 0.5)

    def body(carry):
        i, prefix, cge = carry
        return (i + 2,) + step(i + 1, step(i, (prefix, cge), count_fine), count_fine)

    _, prefix, cge = lax.while_loop(cond, body, (jnp.int32(first_fine),) + state)
    prefix = jnp.maximum(prefix, jnp.int32(KEY_LOWEST_FINITE))
    return _key_to_float(prefix), cge


def _drop_excess_ties(segments, thr, cge, k, key_axis):
    excess = cge > float(k)
    n_eq = _count(segments, lambda tile: tile == thr, thr.shape, key_axis)
    need = float(k) - (cge - n_eq)
    seen = jnp.zeros(thr.shape, F32)
    for ref, n_tiles in segments:
        width = ref.shape[1 + key_axis]
        cw = min(width, 2 * LANES)
        r_i = lax.broadcasted_iota(jnp.int32, (cw, cw), 0)
        c_i = lax.broadcasted_iota(jnp.int32, (cw, cw), 1)
        tri = jnp.where(r_i >= c_i if key_axis == 0 else r_i <= c_i, 1.0, 0.0).astype(BF16)

        def drop_body(j, seen, ref=ref, width=width, cw=cw, tri=tri):
            for c in range(width // cw):
                idx = (j, slice(c * cw, (c + 1) * cw), slice(None)) if key_axis == 0 else (
                    j, slice(None), slice(c * cw, (c + 1) * cw))
                tile = ref[idx]
                eq = tile == thr
                eq_b = jnp.where(eq, 1.0, 0.0).astype(BF16)
                within = (jnp.dot(tri, eq_b, preferred_element_type=F32) if key_axis == 0
                          else jnp.dot(eq_b, tri, preferred_element_type=F32))
                drop = eq & excess & (seen + within > need)
                ref[idx] = jnp.where(drop, -jnp.inf, tile)
                seen = seen + jnp.sum(jnp.where(eq, 1.0, 0.0), axis=key_axis, keepdims=True)
            return seen

        seen = lax.fori_loop(0, n_tiles, drop_body, seen)


def _select_threshold(segments, stat_shape, k, key_axis, coarse=None):
    thr, cge = _kth_largest(segments, stat_shape, k, key_axis, coarse)

    @pl.when(jnp.max(cge) > float(k))
    def _():
        _drop_excess_ties(segments, thr, cge, k, key_axis)

    return thr


def _softmax_step_t(q, kt, vt_aug, sels, m_old, acc_old):
    width = sels[0].shape[1]
    rows = sels[0].shape[0]
    s = lax.dot_general(kt, q, _NT, preferred_element_type=F32)
    s = jnp.concatenate(
        [jnp.concatenate([jnp.where(sel, s[t * rows:(t + 1) * rows, r * width:(r + 1) * width], MASKED_LOGIT)
                          for r in range(q.shape[0] // width)], axis=1)
         for t, sel in enumerate(sels)], axis=0)
    m_new = jnp.maximum(m_old, jnp.max(s, axis=0, keepdims=True))
    p = jnp.exp2(s - m_new)
    alpha = jnp.exp2(m_old - m_new)
    acc_new = alpha * acc_old + jnp.dot(vt_aug, p.astype(BF16), preferred_element_type=F32)
    return m_new, acc_new


def _prompt_attn_kernel(q_ref, qi_ref, wqt_ref, k_ref, vt_ref, ki_ref, kmx_ref, o_ref,
                        s_ref, c_ref, m_ref, acc_ref, *, top_k):
    tq = q_ref.shape[0]
    tk = s_ref.shape[1]
    i = pl.program_id(0)

    qi = qi_ref[...]
    wqt = wqt_ref[...]
    qi_all = jnp.concatenate([qi[:, h * IDX_DIM:(h + 1) * IDX_DIM] for h in range(N_IDX_HEADS)], axis=0)
    w_all = jnp.concatenate([wqt[h:h + 1, :] for h in range(N_IDX_HEADS)], axis=1)

    def scores(kt):
        terms = jnp.maximum(lax.dot_general(kt, qi_all, _NT, preferred_element_type=F32), 0.0) * w_all
        acc = terms[:, 0:tq]
        for h in range(1, N_IDX_HEADS):
            acc = acc + terms[:, h * tq:(h + 1) * tq]
        return acc + 0.0

    def key_rows(j):
        return pl.ds(pl.multiple_of(j * tk, tk), tk)

    def put_scores(j, sc):
        s_ref[j] = sc
        c_ref[j] = _truncate_to_bf16(sc)

    span = vt_ref.shape[3] // tk

    def score_span_body(n, carry):
        rows = pl.ds(pl.multiple_of(n * (span * tk), span * tk), span * tk)
        sc = scores(ki_ref[rows, :])
        for t in range(span):
            put_scores(n * span + t, sc[t * tk:(t + 1) * tk])
        return carry

    def score_body(j, carry):
        put_scores(j, scores(ki_ref[key_rows(j), :]))
        return carry

    lax.fori_loop(0, i // span, score_span_body, 0)
    lax.fori_loop((i // span) * span, i, score_body, 0)
    visible = (lax.broadcasted_iota(jnp.int32, (tk, tq), 0)
               <= lax.broadcasted_iota(jnp.int32, (tk, tq), 1))
    put_scores(i, jnp.where(visible, scores(ki_ref[key_rows(i), :]), -jnp.inf))

    thr = _select_threshold([(s_ref, i + 1)], (1, tq), top_k, key_axis=0, coarse=[(c_ref, i + 1)])

    n_steps = (i + span) // span
    for extra in range(1, span):
        @pl.when(i + extra < n_steps * span)
        def _():
            s_ref[i + extra] = jnp.full((tk, tq), -jnp.inf, F32)
    acc_ref[...] = jnp.zeros(acc_ref.shape, F32)

    q_groups = [jnp.concatenate([q_ref[:, h * HEAD_DIM:(h + 1) * HEAD_DIM]
                                 for h in range(g * GROUP, (g + 1) * GROUP)], axis=0)
                for g in range(N_KV_HEADS)]

    ones = jnp.ones((SUBLANES, HEAD_DIM), BF16)

    def sq_norms(h):
        qf = q_ref[:, h * HEAD_DIM:(h + 1) * HEAD_DIM].astype(F32)
        return lax.dot_general(ones, (qf * qf).astype(BF16), _NT, preferred_element_type=F32)[0:1]

    bounds = [jnp.sqrt(jnp.concatenate([sq_norms(h) for h in range(g * GROUP, (g + 1) * GROUP)], axis=1)
                       * kmx_ref[g:g + 1, 0:1]) * BOUND_PAD
              for g in range(N_KV_HEADS)]
    bounded = jnp.max(jnp.maximum(bounds[0], bounds[1])) <= MAX_STATIC_BOUND

    def key_span(n):
        return pl.ds(pl.multiple_of(n * (span * tk), span * tk), span * tk)

    def bounded_body(n, carry):
        masks = [jnp.where(s_ref[n * span + t] >= thr, 1.0, 0.0).astype(BF16) for t in range(span)]
        mask = jnp.concatenate([jnp.concatenate([mk] * GROUP, axis=1) for mk in masks], axis=0)
        for g in range(N_KV_HEADS):
            s = lax.dot_general(k_ref[key_span(n), g * HEAD_DIM:(g + 1) * HEAD_DIM], q_groups[g], _NT,
                                preferred_element_type=F32)
            p = jnp.exp2(s - bounds[g]).astype(BF16) * mask
            acc_ref[g] = acc_ref[g] + jnp.dot(vt_ref[n, g], p, preferred_element_type=F32)
        return carry

    def running_max_body(n, carry):
        sels = [s_ref[n * span + t] >= thr for t in range(span)]
        for g in range(N_KV_HEADS):
            m_new, acc_new = _softmax_step_t(
                q_groups[g], k_ref[key_span(n), g * HEAD_DIM:(g + 1) * HEAD_DIM], vt_ref[n, g], sels,
                m_ref[g], acc_ref[g])
            m_ref[g] = m_new
            acc_ref[g] = acc_new
        return carry

    @pl.when(bounded)
    def _():
        lax.fori_loop(0, n_steps, bounded_body, 0)

    @pl.when(jnp.logical_not(bounded))
    def _():
        m_ref[...] = jnp.full(m_ref.shape, MASKED_LOGIT, F32)
        lax.fori_loop(0, n_steps, running_max_body, 0)

    for g in range(N_KV_HEADS):
        acc = acc_ref[g]
        out_t = acc[0:HEAD_DIM] * (1.0 / acc[HEAD_DIM:HEAD_DIM + 1])
        for r in range(GROUP):
            h = g * GROUP + r
            o_ref[:, h * HEAD_DIM:(h + 1) * HEAD_DIM] = out_t[:, r * tq:(r + 1) * tq].T


def _prompt_attention(qb, qib, wq, kb, vb, kib, kmx):
    t = qb.shape[0]
    tq = min(2 * LANES, t)
    assert t % tq == 0
    nt = t // tq
    span = 4 if nt % 4 == 0 else (2 if nt % 2 == 0 else 1)
    top_k = min(TOPK_MAX, t // 4)
    vt = vb.reshape(nt // span, span * tq, N_KV_HEADS, HEAD_DIM).transpose(0, 2, 3, 1)
    vt = jnp.concatenate([vt, jnp.ones((nt // span, N_KV_HEADS, SUBLANES, span * tq), BF16)], axis=2)
    row = lambda i: (i, 0)
    const = lambda i: (0, 0)
    return pl.pallas_call(
        functools.partial(_prompt_attn_kernel, top_k=top_k),
        grid=(nt,),
        in_specs=[
            pl.BlockSpec((tq, ATT_DIM), row),
            pl.BlockSpec((tq, QI_DIM), row),
            pl.BlockSpec((N_IDX_HEADS, tq), lambda i: (0, i)),
            pl.BlockSpec((t, KV_DIM), const),
            pl.BlockSpec(vt.shape, lambda i: (0, 0, 0, 0)),
            pl.BlockSpec((t, IDX_DIM), const),
            pl.BlockSpec((SUBLANES, LANES), const),
        ],
        out_specs=pl.BlockSpec((tq, ATT_DIM), row),
        out_shape=jax.ShapeDtypeStruct((t, ATT_DIM), F32),
        scratch_shapes=[
            pltpu.VMEM((nt, tq, tq), F32),
            pltpu.VMEM((nt, tq, tq), BF16),
            pltpu.VMEM((N_KV_HEADS, 1, GROUP * tq), F32),
            pltpu.VMEM((N_KV_HEADS, HEAD_DIM + SUBLANES, GROUP * tq), F32),
        ],
        compiler_params=pltpu.CompilerParams(
            dimension_semantics=("arbitrary",), vmem_limit_bytes=V7X_VMEM_LIMIT_BYTES),
        name="attn_prompt",
    )(qb, qib, wq.T, kb, vt, kib, kmx)


def _softmax_step(q, kt, vt, sel, m_old, l_old, acc_old):
    s = lax.dot_general(q, kt, _NT, preferred_element_type=F32)
    s = jnp.where(sel, s, MASKED_LOGIT)
    m_new = jnp.maximum(m_old, jnp.max(s, axis=-1, keepdims=True))
    p = jnp.exp2(s - m_new)
    alpha = jnp.exp2(m_old - m_new)
    l_new = alpha * l_old + jnp.sum(p, axis=-1, keepdims=True)
    acc_new = alpha * acc_old + jnp.dot(p.astype(BF16), vt, preferred_element_type=F32)
    return m_new, l_new, acc_new


def _sample_attn_kernel(pt_ref, qi_ref, wq_ref, q_ref, kin_ref, kn_ref, vn_ref,
                        ck_hbm, cv_hbm, ckit_hbm, o_ref,
                        kbuf, vbuf, kitbuf, s_main, s_tail, sem, *, layer, top_k, pages_per_chunk):
    b = pl.program_id(0)
    n_chunks, n_tok, chunk = s_main.shape
    page = chunk // pages_per_chunk

    def page_id(c, p):
        return pt_ref[b, c * pages_per_chunk + p]

    def kv_copies(hbm, buf, sem_row, c, slot):
        rows = page * N_KV_HEADS
        return [pltpu.make_async_copy(hbm.at[layer, page_id(c, p)],
                                      buf.at[slot, pl.ds(p * rows, rows)], sem.at[sem_row, slot])
                for p in range(pages_per_chunk)]

    def kit_copies(c, slot):
        return [pltpu.make_async_copy(ckit_hbm.at[layer, page_id(c, p)],
                                      kitbuf.at[slot, :, pl.ds(p * page, page)], sem.at[0, slot])
                for p in range(pages_per_chunk)]

    def start(copies):
        for cp in copies:
            cp.start()

    def wait(copies):
        for cp in copies:
            cp.wait()

    qi = qi_ref[0]
    wq = wq_ref[0]
    w_cols = [wq[:, h:h + 1] for h in range(N_IDX_HEADS)]

    def head_sum(s_all):
        acc = None
        for h in range(N_IDX_HEADS):
            term = jnp.maximum(s_all[h * n_tok:(h + 1) * n_tok, :], 0.0) * w_cols[h]
            acc = term if acc is None else acc + term
        return acc + 0.0

    start(kit_copies(0, 0))

    def score_body(c, carry):
        slot = c % 2

        @pl.when(c + 1 < n_chunks)
        def _():
            start(kit_copies(c + 1, 1 - slot))

        wait(kit_copies(c, slot))
        s_main[c] = head_sum(jnp.dot(qi, kitbuf[slot].astype(BF16), preferred_element_type=F32))
        return carry

    lax.fori_loop(0, n_chunks, score_body, 0)
    col = lax.broadcasted_iota(jnp.int32, (n_tok, LANES), 1)
    tok = lax.broadcasted_iota(jnp.int32, (n_tok, LANES), 0)
    s_new = head_sum(lax.dot_general(qi, kin_ref[0], _NT, preferred_element_type=F32))
    s_tail[0] = jnp.where(col <= tok, s_new, -jnp.inf)

    start(kv_copies(ck_hbm, kbuf, 1, 0, 0))
    start(kv_copies(cv_hbm, vbuf, 2, 0, 0))

    thr = _select_threshold([(s_main, n_chunks), (s_tail, 1)], (n_tok, 1), top_k, key_axis=1)

    rows = GROUP * n_tok
    thr_rows = jnp.concatenate([thr] * GROUP, axis=0)

    def attend(state, kv_of, s_tok):
        sel = jnp.concatenate([s_tok] * GROUP, axis=0) >= thr_rows
        new_state = []
        for g in range(N_KV_HEADS):
            kt, vt = kv_of(g)
            new_state.append(_softmax_step(q_ref[0, g], kt, vt, sel, *state[g]))
        return tuple(new_state)

    def attend_body(c, state):
        slot = c % 2

        @pl.when(c + 1 < n_chunks)
        def _():
            start(kv_copies(ck_hbm, kbuf, 1, c + 1, 1 - slot))
            start(kv_copies(cv_hbm, vbuf, 2, c + 1, 1 - slot))

        wait(kv_copies(ck_hbm, kbuf, 1, c, slot))
        wait(kv_copies(cv_hbm, vbuf, 2, c, slot))

        def kv_of(g):
            head_rows = pl.ds(g, chunk, stride=N_KV_HEADS)
            return kbuf[slot, head_rows, :].astype(BF16), vbuf[slot, head_rows, :].astype(BF16)

        return attend(state, kv_of, s_main[c])

    init = tuple((jnp.full((rows, 1), MASKED_LOGIT, F32), jnp.zeros((rows, 1), F32),
                  jnp.zeros((rows, HEAD_DIM), F32)) for _ in range(N_KV_HEADS))
    state = lax.fori_loop(0, n_chunks, attend_body, init)

    def new_kv(g):
        cols = slice(g * HEAD_DIM, (g + 1) * HEAD_DIM)
        return kn_ref[0, :, cols], vn_ref[0, :, cols]

    state = attend(state, new_kv, s_tail[0])
    for g in range(N_KV_HEADS):
        _, l_fin, acc_fin = state[g]
        o_ref[0, g] = acc_fin * (1.0 / l_fin)


def _sample_attention(qb, qib, wq, kb, vb, kib, cache_k_rows, cache_v_rows, cache_kidx_t, page_table, layer):
    nb, n_pages = page_table.shape
    n_tok = qb.shape[0] // nb
    page = cache_kidx_t.shape[3]
    assert n_tok == SUBLANES and n_tok <= LANES
    pages_per_chunk = 16
    while n_pages % pages_per_chunk:
        pages_per_chunk //= 2
    n_chunks = n_pages // pages_per_chunk
    chunk = pages_per_chunk * page
    top_k = min(TOPK_MAX, (n_pages * page + n_tok) // 4)

    q_s = qb.reshape(nb, n_tok, N_KV_HEADS, GROUP, HEAD_DIM).transpose(0, 2, 3, 1, 4)
    q_s = q_s.reshape(nb, N_KV_HEADS, GROUP * n_tok, HEAD_DIM)
    qi_s = qib.reshape(nb, n_tok, N_IDX_HEADS, IDX_DIM).transpose(0, 2, 1, 3)
    qi_s = qi_s.reshape(nb, N_IDX_HEADS * n_tok, IDX_DIM)
    wq_s = wq.reshape(nb, n_tok, N_IDX_HEADS)
    pad = lambda a: jnp.pad(a.reshape(nb, n_tok, a.shape[-1]), ((0, 0), (0, LANES - n_tok), (0, 0)))
    kin_s, kn_s, vn_s = pad(kib), pad(kb), pad(vb)

    seq3 = lambda b, pt: (b, 0, 0)
    seq4 = lambda b, pt: (b, 0, 0, 0)
    out = pl.pallas_call(
        functools.partial(_sample_attn_kernel, layer=layer, top_k=top_k,
                          pages_per_chunk=pages_per_chunk),
        grid_spec=pltpu.PrefetchScalarGridSpec(
            num_scalar_prefetch=1,
            grid=(nb,),
            in_specs=[
                pl.BlockSpec((1, N_IDX_HEADS * n_tok, IDX_DIM), seq3),
                pl.BlockSpec((1, n_tok, N_IDX_HEADS), seq3),
                pl.BlockSpec((1, N_KV_HEADS, GROUP * n_tok, HEAD_DIM), seq4),
                pl.BlockSpec((1, LANES, IDX_DIM), seq3),
                pl.BlockSpec((1, LANES, KV_DIM), seq3),
                pl.BlockSpec((1, LANES, KV_DIM), seq3),
                pl.BlockSpec(memory_space=pl.ANY),
                pl.BlockSpec(memory_space=pl.ANY),
                pl.BlockSpec(memory_space=pl.ANY),
            ],
            out_specs=pl.BlockSpec((1, N_KV_HEADS, GROUP * n_tok, HEAD_DIM), seq4),
            scratch_shapes=[
                pltpu.VMEM((2, chunk * N_KV_HEADS, HEAD_DIM), F32),
                pltpu.VMEM((2, chunk * N_KV_HEADS, HEAD_DIM), F32),
                pltpu.VMEM((2, IDX_DIM, chunk), F32),
                pltpu.VMEM((n_chunks, n_tok, chunk), F32),
                pltpu.VMEM((1, n_tok, LANES), F32),
                pltpu.SemaphoreType.DMA((3, 2)),
            ],
        ),
        out_shape=jax.ShapeDtypeStruct((nb, N_KV_HEADS, GROUP * n_tok, HEAD_DIM), F32),
        compiler_params=pltpu.CompilerParams(
            dimension_semantics=("arbitrary",), vmem_limit_bytes=V7X_VMEM_LIMIT_BYTES),
        name="attn_sample",
    )(page_table, qi_s, wq_s, q_s, kin_s, kn_s, vn_s, cache_k_rows, cache_v_rows, cache_kidx_t)
    out = out.reshape(nb, N_KV_HEADS, GROUP, n_tok, HEAD_DIM).transpose(0, 3, 1, 2, 4)
    return out.reshape(nb * n_tok, ATT_DIM)


def _pack_w_in(w):
    ends, off = [], 0
    for size in (CONV_DIM, CONV_DIM, CONV_DIM, CONV_DIM, ATT_DIM, KV_DIM, KV_DIM, QI_DIM, IDX_DIM,
                 N_IDX_HEADS, ATT_DIM, D_MODEL, D_MODEL):
        ends.append((off, off + size))
        off += size
    (uh, ub, uc, za, q, k, v, qi, ki, wi, zb, ga, gb) = [w[:, a:b] for a, b in ends]
    pad = jnp.zeros((w.shape[0], LANES - IDX_DIM - N_IDX_HEADS), w.dtype)
    return jnp.concatenate([uh, ub, uc, za, q, k, v, qi, zb, ga, gb, ki, wi, pad], axis=1).astype(BF16)


def kernel(x_prompt, x_sample, c_prompt, c_sample, cache_k, cache_v, cache_kidx, state_conv, page_table,
           norm_g, w_mod, b_mod, w_in, conv_w, q_norm, k_norm, w_proj_a, w_proj_b, w_out):
    depth = w_in.shape[0]
    bp, seq, _ = x_prompt.shape
    nb, n_tok, _ = x_sample.shape
    assert bp == 1
    n_phys, page = cache_k.shape[1], cache_k.shape[2]
    cache_k = cache_k.reshape(depth, n_phys, page * N_KV_HEADS, HEAD_DIM)
    cache_v = cache_v.reshape(depth, n_phys, page * N_KV_HEADS, HEAD_DIM)
    cache_kidx = jnp.swapaxes(cache_kidx, 2, 3)

    mod = _modulation(jnp.concatenate([c_prompt, c_sample], axis=0), w_mod, b_mod)

    yp = x_prompt.reshape(seq, D_MODEL)
    ys = x_sample.reshape(nb * n_tok, D_MODEL)
    outs = [[] for _ in range(8)]
    for l in range(depth):
        w_in_p = _pack_w_in(w_in[l])
        w_pa, w_pb, w_o = (w_proj_a[l].astype(BF16), w_proj_b[l].astype(BF16), w_out[l].astype(BF16))
        ng, qn, kn = norm_g[l][None, :], q_norm[l][None, :], k_norm[l][None, :]

        shift, scale, gate = jnp.split(mod[l, 0:bp], 3, axis=-1)
        (ap, qb, kf, kb, vf, vb, qib, kif, kib, wq, zb, sgb, ust, kmx) = _projection(
            yp, ng, scale, shift, w_in_p, conv_w[l], qn, kn, w_pa)
        attn = _prompt_attention(qb, qib, wq, kb, vb, kib, kmx)
        yp = _output(ap, attn, zb, sgb, yp, gate, w_pb, w_o)
        outs[0].append(kf.reshape(bp, seq, N_KV_HEADS, HEAD_DIM))
        outs[1].append(vf.reshape(bp, seq, N_KV_HEADS, HEAD_DIM))
        outs[2].append(kif.reshape(bp, seq, IDX_DIM))
        outs[3].append(ust[SUBLANES - (CONV_WIDTH - 1):].reshape(bp, CONV_WIDTH - 1, CONV_DIM))

        shift, scale, gate = [jnp.repeat(m, n_tok, axis=0) for m in jnp.split(mod[l, bp:], 3, axis=-1)]
        st = state_conv[l]
        zeros = jnp.zeros((nb, n_tok - 2, CONV_DIM), F32)
        p1 = jnp.concatenate([st[:, 1:2], st[:, 1:2], zeros], axis=1).reshape(nb * n_tok, CONV_DIM)
        p2 = jnp.concatenate([st[:, 0:1], st[:, 1:2], zeros], axis=1).reshape(nb * n_tok, CONV_DIM)
        (ap, qb, kf, kb, vf, vb, qib, kif, kib, wq, zb, sgb, ust, _) = _projection(
            ys, ng, scale, shift, w_in_p, conv_w[l], qn, kn, w_pa, prev=(p1, p2), seg_len=n_tok)
        attn = _sample_attention(qb, qib, wq, kb, vb, kib, cache_k, cache_v, cache_kidx, page_table, l)
        ys = _output(ap, attn, zb, sgb, ys, gate, w_pb, w_o)
        outs[4].append(kf.reshape(nb, n_tok, N_KV_HEADS, HEAD_DIM))
        outs[5].append(vf.reshape(nb, n_tok, N_KV_HEADS, HEAD_DIM))
        outs[6].append(kif.reshape(nb, n_tok, IDX_DIM))
        outs[7].append(ust.reshape(nb, n_tok, CONV_DIM)[:, n_tok - (CONV_WIDTH - 1):])

    return (yp.reshape(bp, seq, D_MODEL), ys.reshape(nb, n_tok, D_MODEL),
            *[jnp.stack(o) for o in outs])
```

```python
import functools

import jax
import jax.numpy as jnp
from jax import lax
from jax.experimental import pallas as pl
from jax.experimental.pallas import tpu as pltpu

F32 = jnp.float32
BF16 = jnp.bfloat16

D_MODEL = 1024
CONV_DIM = 1024
CONV_WIDTH = 3
N_HEADS = 8
N_KV_HEADS = 2
HEAD_DIM = 128
ATT_DIM = N_HEADS * HEAD_DIM
KV_DIM = N_KV_HEADS * HEAD_DIM
GROUP = N_HEADS // N_KV_HEADS
N_IDX_HEADS = 8
IDX_DIM = 64
QI_DIM = N_IDX_HEADS * IDX_DIM
TOPK_MAX = 256
EPS = 1e-6
MASKED_LOGIT = -1e30
LOG2_E = 1.4426950408889634
MAX_STATIC_BOUND = 50.0
BOUND_PAD = 1.02

LANES = 128
SUBLANES = 8
V7X_VMEM_LIMIT_BYTES = 56 * 1024 * 1024

OFF_UH = 0
OFF_UB = OFF_UH + CONV_DIM
OFF_UC = OFF_UB + CONV_DIM
OFF_ZA = OFF_UC + CONV_DIM
OFF_Q = OFF_ZA + CONV_DIM
OFF_K = OFF_Q + ATT_DIM
OFF_V = OFF_K + KV_DIM
OFF_QI = OFF_V + KV_DIM
OFF_ZB = OFF_QI + QI_DIM
OFF_GA = OFF_ZB + ATT_DIM
OFF_GB = OFF_GA + D_MODEL
OFF_KW = OFF_GB + D_MODEL
P_COLS = OFF_KW + LANES

KEY_LOWEST_FINITE = -2139095040
INT_MIN = -(2 ** 31)

_NT = (((1,), (1,)), ((), ()))


def _silu(x):
    return x * jax.nn.sigmoid(x)


def _rms_rows(x, g):
    ms = jnp.mean(x * x, axis=-1, keepdims=True)
    return (x * lax.rsqrt(ms + EPS)) * g


def _mod_kernel(c_ref, w_ref, b_ref, o_ref):
    s = _silu(c_ref[...]).astype(BF16)
    o_ref[0] = jnp.dot(s, w_ref[0].astype(BF16), preferred_element_type=F32) + b_ref[0]


def _modulation(c_all, w_mod, b_mod):
    depth = w_mod.shape[0]
    nb = c_all.shape[0]
    return pl.pallas_call(
        _mod_kernel,
        grid=(depth,),
        in_specs=[
            pl.BlockSpec((nb, D_MODEL), lambda l: (0, 0)),
            pl.BlockSpec((1, D_MODEL, 3 * D_MODEL), lambda l: (l, 0, 0)),
            pl.BlockSpec((1, 1, 3 * D_MODEL), lambda l: (l, 0, 0)),
        ],
        out_specs=pl.BlockSpec((1, nb, 3 * D_MODEL), lambda l: (l, 0, 0)),
        out_shape=jax.ShapeDtypeStruct((depth, nb, 3 * D_MODEL), F32),
        compiler_params=pltpu.CompilerParams(
            dimension_semantics=("arbitrary",), vmem_limit_bytes=V7X_VMEM_LIMIT_BYTES),
        name="adaln_mod",
    )(c_all, w_mod, b_mod.reshape(depth, 1, 3 * D_MODEL))


def _proj_kernel(*refs, seg_len):
    if seg_len is None:
        (x_ref, ng_ref, sc_ref, sh_ref, w_ref, cw_ref, qn_ref, kn_ref, wpa_ref,
         ap_ref, qb_ref, kf_ref, kb_ref, vf_ref, vb_ref, qib_ref, kif_ref, kib_ref, wq_ref,
         zb_ref, sgb_ref, ust_ref, kmx_ref, carry_ref) = refs
    else:
        (x_ref, ng_ref, sc_ref, sh_ref, w_ref, cw_ref, qn_ref, kn_ref, wpa_ref, p1_ref, p2_ref,
         ap_ref, qb_ref, kf_ref, kb_ref, vf_ref, vb_ref, qib_ref, kif_ref, kib_ref, wq_ref,
         zb_ref, sgb_ref, ust_ref, kmx_ref) = refs
    tm = x_ref.shape[0]

    h = _rms_rows(x_ref[...], ng_ref[...]) * (1.0 + sc_ref[...]) + sh_ref[...]
    hb = h.astype(BF16)

    def proj(off, width):
        return jnp.dot(hb, w_ref[:, off:off + width], preferred_element_type=F32)

    u = proj(OFF_UC, CONV_DIM) * proj(OFF_UH, CONV_DIM)
    row = lax.broadcasted_iota(jnp.int32, (tm, 1), 0)
    if seg_len is None:
        @pl.when(pl.program_id(0) == 0)
        def _():
            carry_ref[...] = jnp.zeros_like(carry_ref)
        prev2 = carry_ref[SUBLANES - 2:SUBLANES - 1, :]
        prev1 = carry_ref[SUBLANES - 1:SUBLANES, :]
        pos = row
        p1 = prev1
        p2 = jnp.where(pos == 0, prev2, prev1)
    else:
        pos = row % seg_len
        p1 = p1_ref[...]
        p2 = p2_ref[...]
    u1 = jnp.where(pos == 0, p1, pltpu.roll(u, 1, 0))
    u2 = jnp.where(pos < 2, p2, pltpu.roll(u, 2, 0))
    conv = (cw_ref[0:1, :] * u2 + cw_ref[1:2, :] * u1) + cw_ref[2:3, :] * u
    if seg_len is None:
        carry_ref[...] = u[tm - SUBLANES:tm, :]
        ust_ref[...] = u[tm - SUBLANES:tm, :]
    else:
        ust_ref[...] = u
    branch_a = (proj(OFF_UB, CONV_DIM) * conv) * _silu(proj(OFF_ZA, CONV_DIM))
    ap_ref[...] = jax.nn.sigmoid(proj(OFF_GA, D_MODEL)) * jnp.dot(
        branch_a.astype(BF16), wpa_ref[...], preferred_element_type=F32)

    qf = proj(OFF_Q, ATT_DIM)
    for hd in range(N_HEADS):
        sl = slice(hd * HEAD_DIM, (hd + 1) * HEAD_DIM)
        qb_ref[:, sl] = (_rms_rows(qf[:, sl], qn_ref[...]) * (HEAD_DIM ** -0.5 * LOG2_E)).astype(BF16)
    kf = proj(OFF_K, KV_DIM)

    @pl.when(pl.program_id(0) == 0)
    def _():
        kmx_ref[...] = jnp.zeros_like(kmx_ref)

    for g in range(N_KV_HEADS):
        sl = slice(g * HEAD_DIM, (g + 1) * HEAD_DIM)
        kn = _rms_rows(kf[:, sl], kn_ref[...])
        kf_ref[:, sl] = kn
        kb_ref[:, sl] = kn.astype(BF16)
        kr = kn.astype(BF16).astype(F32)
        n2 = jnp.max(jnp.sum(kr * kr, axis=-1, keepdims=True), axis=0, keepdims=True)
        kmx_ref[g:g + 1, :] = jnp.maximum(kmx_ref[g:g + 1, :], n2)
    vf = proj(OFF_V, KV_DIM)
    vf_ref[...] = vf
    vb_ref[...] = vf.astype(BF16)
    qib_ref[...] = proj(OFF_QI, QI_DIM).astype(BF16)
    kw = proj(OFF_KW, LANES)
    kif_ref[...] = kw[:, 0:IDX_DIM]
    kib_ref[...] = kw[:, 0:IDX_DIM].astype(BF16)
    wq_ref[...] = (kw[:, IDX_DIM:IDX_DIM + N_IDX_HEADS] * (N_IDX_HEADS ** -0.5)) * (IDX_DIM ** -0.5)
    zb_ref[...] = _silu(proj(OFF_ZB, ATT_DIM))
    sgb_ref[...] = jax.nn.sigmoid(proj(OFF_GB, D_MODEL))


def _row_tile(t):
    for tm in (512, 256, 128, 64, 32, 16, 8):
        if t % tm == 0:
            return tm
    raise ValueError(f"row count {t} is not a multiple of {SUBLANES}")


def _projection(x, ng, scale, shift, w_in_p, conv_w, qn, kn, w_pa, prev=None, seg_len=None):
    t = x.shape[0]
    tm = _row_tile(t) if seg_len is None else t
    gr = scale.shape[0]
    assert gr in (1, t)
    row = lambda i: (i, 0)
    const = lambda i: (0, 0)
    mod_spec = pl.BlockSpec((1 if gr == 1 else tm, D_MODEL), const if gr == 1 else row)
    in_specs = [
        pl.BlockSpec((tm, D_MODEL), row),
        pl.BlockSpec((1, D_MODEL), const),
        mod_spec, mod_spec,
        pl.BlockSpec((D_MODEL, P_COLS), const),
        pl.BlockSpec((CONV_WIDTH, CONV_DIM), const),
        pl.BlockSpec((1, HEAD_DIM), const),
        pl.BlockSpec((1, HEAD_DIM), const),
        pl.BlockSpec((CONV_DIM, D_MODEL), const),
    ]
    args = [x, ng, scale, shift, w_in_p, conv_w, qn, kn, w_pa]
    scratch = []
    if seg_len is None:
        ust_shape, ust_spec = (SUBLANES, CONV_DIM), pl.BlockSpec((SUBLANES, CONV_DIM), const)
        scratch.append(pltpu.VMEM((SUBLANES, CONV_DIM), F32))
    else:
        in_specs += [pl.BlockSpec((tm, CONV_DIM), row)] * 2
        args += list(prev)
        ust_shape, ust_spec = (t, CONV_DIM), pl.BlockSpec((tm, CONV_DIM), row)
    widths = [(D_MODEL, F32), (ATT_DIM, BF16), (KV_DIM, F32), (KV_DIM, BF16), (KV_DIM, F32),
              (KV_DIM, BF16), (QI_DIM, BF16), (IDX_DIM, F32), (IDX_DIM, BF16), (N_IDX_HEADS, F32),
              (ATT_DIM, F32), (D_MODEL, F32)]
    out_shape = [jax.ShapeDtypeStruct((t, w), dt) for w, dt in widths]
    out_specs = [pl.BlockSpec((tm, w), row) for w, _ in widths]
    out_shape.append(jax.ShapeDtypeStruct(ust_shape, F32))
    out_specs.append(ust_spec)
    out_shape.append(jax.ShapeDtypeStruct((SUBLANES, LANES), F32))
    out_specs.append(pl.BlockSpec((SUBLANES, LANES), const))
    return pl.pallas_call(
        functools.partial(_proj_kernel, seg_len=seg_len),
        grid=(t // tm,),
        in_specs=in_specs,
        out_specs=out_specs,
        out_shape=out_shape,
        scratch_shapes=scratch,
        compiler_params=pltpu.CompilerParams(
            dimension_semantics=("arbitrary",), vmem_limit_bytes=V7X_VMEM_LIMIT_BYTES),
        name="proj_prompt" if seg_len is None else "proj_sample",
    )(*args)


def _out_kernel(ap_ref, at_ref, zb_ref, sgb_ref, x_ref, gate_ref, wpb_ref, wout_ref, y_ref):
    branch_b = (at_ref[...] * zb_ref[...]).astype(BF16)
    merged = ap_ref[...] + sgb_ref[...] * jnp.dot(branch_b, wpb_ref[...], preferred_element_type=F32)
    y_ref[...] = x_ref[...] + gate_ref[...] * jnp.dot(
        merged.astype(BF16), wout_ref[...], preferred_element_type=F32)


def _output(a_part, attn, zb, sgb, x, gate, w_pb, w_out):
    t = x.shape[0]
    tm = _row_tile(t)
    gr = gate.shape[0]
    assert gr in (1, t)
    row = lambda i: (i, 0)
    const = lambda i: (0, 0)
    act = pl.BlockSpec((tm, D_MODEL), row)
    return pl.pallas_call(
        _out_kernel,
        grid=(t // tm,),
        in_specs=[act, act, act, act, act,
                  pl.BlockSpec((1 if gr == 1 else tm, D_MODEL), const if gr == 1 else row),
                  pl.BlockSpec((ATT_DIM, D_MODEL), const),
                  pl.BlockSpec((D_MODEL, D_MODEL), const)],
        out_specs=act,
        out_shape=jax.ShapeDtypeStruct((t, D_MODEL), F32),
        compiler_params=pltpu.CompilerParams(
            dimension_semantics=("arbitrary",), vmem_limit_bytes=V7X_VMEM_LIMIT_BYTES),
        name="out_proj",
    )(a_part, attn, zb, sgb, x, gate, w_pb, w_out)


def _key_to_float(key):
    bits = jnp.where(key >= 0, key, key ^ jnp.int32(0x7FFFFFFF))
    return lax.bitcast_convert_type(bits, F32)


def _tree_sum(parts):
    while len(parts) > 1:
        parts = [parts[i] + parts[i + 1] for i in range(0, len(parts) - 1, 2)] + (
            [parts[-1]] if len(parts) % 2 else [])
    return parts[0]


def _fold_keys(x, key_axis, rows=SUBLANES):
    if key_axis == 0:
        return _tree_sum([x[c * rows:(c + 1) * rows] for c in range(x.shape[0] // rows)])
    return _tree_sum([x[:, c * LANES:(c + 1) * LANES] for c in range(x.shape[1] // LANES)])


COUNT_UNROLL = 4


def _loop_tiles(n_tiles, body, init):
    if isinstance(n_tiles, int) and n_tiles <= 16:
        for j in range(n_tiles):
            init = body(j, init)
        return init

    def group_body(n, carry):
        for t in range(COUNT_UNROLL):
            carry = body(n * COUNT_UNROLL + t, carry)
        return carry

    n_groups = n_tiles // COUNT_UNROLL
    init = lax.fori_loop(0, n_groups, group_body, init)
    return lax.fori_loop(n_groups * COUNT_UNROLL, n_tiles, body, init)


def _count(segments, pred, stat_shape, key_axis):
    fold_shape = (SUBLANES, stat_shape[1]) if key_axis == 0 else (stat_shape[0], LANES)
    total = jnp.zeros(fold_shape, F32)
    for ref, n_tiles in segments:
        def body(j, cnt, ref=ref):
            return cnt + _fold_keys(jnp.where(pred(ref[j]), 1.0, 0.0), key_axis)
        total = _loop_tiles(n_tiles, body, total)
    return jnp.sum(total, axis=key_axis, keepdims=True)


def _truncate_to_bf16(x):
    bits = lax.bitcast_convert_type(x, jnp.int32) & jnp.int32(-65536)
    return lax.bitcast_convert_type(bits, F32).astype(BF16)


def _count_coarse(coarse, thr):
    pack = 2 * SUBLANES
    total = jnp.zeros((pack, thr.shape[1]), F32)
    one, zero = jnp.ones((), thr.dtype), jnp.zeros((), thr.dtype)
    for ref, n_tiles in coarse:
        def body(j, cnt, ref=ref):
            hit = jnp.where(ref[j] >= thr, one, zero)
            return cnt + _fold_keys(hit, 0, rows=pack).astype(F32)
        total = _loop_tiles(n_tiles, body, total)
    return jnp.sum(total, axis=0, keepdims=True)


def _merge_two_smallest(a, b):
    return jnp.minimum(a[0], b[0]), jnp.minimum(jnp.maximum(a[0], b[0]), jnp.minimum(a[1], b[1]))


def _two_smallest(segments, thr, key_axis):
    fold_shape = (SUBLANES, thr.shape[1]) if key_axis == 0 else (thr.shape[0], LANES)
    best = (jnp.full(fold_shape, jnp.inf, F32), jnp.full(fold_shape, jnp.inf, F32))
    for ref, n_tiles in segments:
        def body(j, carry, ref=ref):
            tile = ref[j]
            x = jnp.where(tile >= thr, tile, jnp.inf)
            if key_axis == 0:
                parts = [x[c * SUBLANES:(c + 1) * SUBLANES] for c in range(x.shape[0] // SUBLANES)]
            else:
                parts = [x[:, c * LANES:(c + 1) * LANES] for c in range(x.shape[1] // LANES)]
            if len(parts) % 2:
                parts.append(jnp.full(fold_shape, jnp.inf, F32))
            pairs = [(jnp.minimum(parts[c], parts[c + 1]), jnp.maximum(parts[c], parts[c + 1]))
                     for c in range(0, len(parts), 2)]
            while len(pairs) > 1:
                pairs = [_merge_two_smallest(pairs[c], pairs[c + 1]) for c in range(0, len(pairs) - 1, 2)] + (
                    [pairs[-1]] if len(pairs) % 2 else [])
            return _merge_two_smallest(carry, pairs[0])
        best = _loop_tiles(n_tiles, body, best)
    first, second = best
    low1 = jnp.min(first, axis=key_axis, keepdims=True)
    at_low1 = first == low1
    n_low1 = jnp.sum(jnp.where(at_low1, 1.0, 0.0), axis=key_axis, keepdims=True)
    others = jnp.min(jnp.where(at_low1, second, first), axis=key_axis, keepdims=True)
    return low1, jnp.where(n_low1 >= 2.0, low1, others)


def _kth_largest(segments, stat_shape, k, key_axis, coarse=None):
    kf = float(k)

    def step(i, state, count_fn):
        prefix, cge = state
        cand = prefix ^ jnp.left_shift(jnp.int32(1), 31 - i)
        cnt = count_fn(_key_to_float(cand))
        take = cnt >= kf
        return jnp.where(take, cand, prefix), jnp.where(take, cnt, cge)

    def count_fine(t):
        return _count(segments, lambda tile: tile >= t, stat_shape, key_axis)

    state = (jnp.full(stat_shape, INT_MIN, jnp.int32), jnp.full(stat_shape, jnp.inf, F32))
    first_fine = 0
    if coarse is not None:
        assert key_axis == 0
        first_fine = 16
        state = lax.fori_loop(
            0, first_fine,
            lambda i, st: step(i, st, lambda t: _count_coarse(coarse, _truncate_to_bf16(t))), state)

    def cond(carry):
        i, _, cge = carry
        return (i < 32) & (jnp.max(cge) > kf + 1.0)

    def body(carry):
        i, prefix, cge = carry
        return (i + 2,) + step(i + 1, step(i, (prefix, cge), count_fine), count_fine)

    _, prefix, cge = lax.while_loop(cond, body, (jnp.int32(first_fine),) + state)
    thr = _key_to_float(jnp.maximum(prefix, jnp.int32(KEY_LOWEST_FINITE)))
    cge = jnp.where(cge == jnp.inf, 0.0, cge)

    low1, low2 = _two_smallest(segments, thr, key_axis)
    one_over = cge == kf + 1.0
    distinct = one_over & (low2 > low1)
    thr = jnp.where(one_over, jnp.where(distinct, low2, low1), thr)
    return thr, jnp.where(distinct, kf, cge)


def _drop_excess_ties(segments, thr, cge, k, key_axis):
    excess = cge > float(k)
    n_eq = _count(segments, lambda tile: tile == thr, thr.shape, key_axis)
    need = float(k) - (cge - n_eq)
    seen = jnp.zeros(thr.shape, F32)
    for ref, n_tiles in segments:
        width = ref.shape[1 + key_axis]
        cw = min(width, 2 * LANES)
        r_i = lax.broadcasted_iota(jnp.int32, (cw, cw), 0)
        c_i = lax.broadcasted_iota(jnp.int32, (cw, cw), 1)
        tri = jnp.where(r_i >= c_i if key_axis == 0 else r_i <= c_i, 1.0, 0.0).astype(BF16)

        def drop_body(j, seen, ref=ref, width=width, cw=cw, tri=tri):
            for c in range(width // cw):
                idx = (j, slice(c * cw, (c + 1) * cw), slice(None)) if key_axis == 0 else (
                    j, slice(None), slice(c * cw, (c + 1) * cw))
                tile = ref[idx]
                eq = tile == thr
                eq_b = jnp.where(eq, 1.0, 0.0).astype(BF16)
                within = (jnp.dot(tri, eq_b, preferred_element_type=F32) if key_axis == 0
                          else jnp.dot(eq_b, tri, preferred_element_type=F32))
                drop = eq & excess & (seen + within > need)
                ref[idx] = jnp.where(drop, -jnp.inf, tile)
                seen = seen + jnp.sum(jnp.where(eq, 1.0, 0.0), axis=key_axis, keepdims=True)
            return seen

        seen = lax.fori_loop(0, n_tiles, drop_body, seen)


def _select_threshold(segments, stat_shape, k, key_axis, coarse=None):
    thr, cge = _kth_largest(segments, stat_shape, k, key_axis, coarse)

    @pl.when(jnp.max(cge) > float(k))
    def _():
        _drop_excess_ties(segments, thr, cge, k, key_axis)

    return thr


def _softmax_step_t(q, kt, vt_aug, sels, m_old, acc_old):
    width = sels[0].shape[1]
    rows = sels[0].shape[0]
    s = lax.dot_general(kt, q, _NT, preferred_element_type=F32)
    s = jnp.concatenate(
        [jnp.concatenate([jnp.where(sel, s[t * rows:(t + 1) * rows, r * width:(r + 1) * width], MASKED_LOGIT)
                          for r in range(q.shape[0] // width)], axis=1)
         for t, sel in enumerate(sels)], axis=0)
    m_new = jnp.maximum(m_old, jnp.max(s, axis=0, keepdims=True))
    p = jnp.exp2(s - m_new)
    alpha = jnp.exp2(m_old - m_new)
    acc_new = alpha * acc_old + jnp.dot(vt_aug, p.astype(BF16), preferred_element_type=F32)
    return m_new, acc_new


def _prompt_attn_kernel(q_ref, qi_ref, wqt_ref, k_ref, vt_ref, ki_ref, kmx_ref, o_ref,
                        s_ref, c_ref, m_ref, acc_ref, *, top_k):
    tq = q_ref.shape[0]
    tk = s_ref.shape[1]
    i = pl.program_id(0)

    qi = qi_ref[...]
    wqt = wqt_ref[...]
    qi_all = jnp.concatenate([qi[:, h * IDX_DIM:(h + 1) * IDX_DIM] for h in range(N_IDX_HEADS)], axis=0)
    w_all = jnp.concatenate([wqt[h:h + 1, :] for h in range(N_IDX_HEADS)], axis=1)

    def scores(kt):
        terms = jnp.maximum(lax.dot_general(kt, qi_all, _NT, preferred_element_type=F32), 0.0) * w_all
        acc = terms[:, 0:tq]
        for h in range(1, N_IDX_HEADS):
            acc = acc + terms[:, h * tq:(h + 1) * tq]
        return acc + 0.0

    def key_rows(j):
        return pl.ds(pl.multiple_of(j * tk, tk), tk)

    def put_scores(j, sc):
        s_ref[j] = sc
        c_ref[j] = _truncate_to_bf16(sc)

    span = vt_ref.shape[3] // tk

    def score_span_body(n, carry):
        rows = pl.ds(pl.multiple_of(n * (span * tk), span * tk), span * tk)
        sc = scores(ki_ref[rows, :])
        for t in range(span):
            put_scores(n * span + t, sc[t * tk:(t + 1) * tk])
        return carry

    def score_body(j, carry):
        put_scores(j, scores(ki_ref[key_rows(j), :]))
        return carry

    lax.fori_loop(0, i // span, score_span_body, 0)
    lax.fori_loop((i // span) * span, i, score_body, 0)
    visible = (lax.broadcasted_iota(jnp.int32, (tk, tq), 0)
               <= lax.broadcasted_iota(jnp.int32, (tk, tq), 1))
    put_scores(i, jnp.where(visible, scores(ki_ref[key_rows(i), :]), -jnp.inf))

    thr = _select_threshold([(s_ref, i + 1)], (1, tq), top_k, key_axis=0, coarse=[(c_ref, i + 1)])

    n_steps = (i + span) // span
    for extra in range(1, span):
        @pl.when(i + extra < n_steps * span)
        def _():
            s_ref[i + extra] = jnp.full((tk, tq), -jnp.inf, F32)
    acc_ref[...] = jnp.zeros(acc_ref.shape, F32)

    q_groups = [jnp.concatenate([q_ref[:, h * HEAD_DIM:(h + 1) * HEAD_DIM]
                                 for h in range(g * GROUP, (g + 1) * GROUP)], axis=0)
                for g in range(N_KV_HEADS)]

    ones = jnp.ones((SUBLANES, HEAD_DIM), BF16)

    def sq_norms(h):
        qf = q_ref[:, h * HEAD_DIM:(h + 1) * HEAD_DIM].astype(F32)
        return lax.dot_general(ones, (qf * qf).astype(BF16), _NT, preferred_element_type=F32)[0:1]

    bounds = [jnp.sqrt(jnp.concatenate([sq_norms(h) for h in range(g * GROUP, (g + 1) * GROUP)], axis=1)
                       * kmx_ref[g:g + 1, 0:1]) * BOUND_PAD
              for g in range(N_KV_HEADS)]
    bounded = jnp.max(jnp.maximum(bounds[0], bounds[1])) <= MAX_STATIC_BOUND

    def key_span(n):
        return pl.ds(pl.multiple_of(n * (span * tk), span * tk), span * tk)

    def bounded_body(n, carry):
        masks = [jnp.where(s_ref[n * span + t] >= thr, 1.0, 0.0).astype(BF16) for t in range(span)]
        mask = jnp.concatenate([jnp.concatenate([mk] * GROUP, axis=1) for mk in masks], axis=0)
        for g in range(N_KV_HEADS):
            s = lax.dot_general(k_ref[key_span(n), g * HEAD_DIM:(g + 1) * HEAD_DIM], q_groups[g], _NT,
                                preferred_element_type=F32)
            p = jnp.exp2(s - bounds[g]).astype(BF16) * mask
            acc_ref[g] = acc_ref[g] + jnp.dot(vt_ref[n, g], p, preferred_element_type=F32)
        return carry

    def running_max_body(n, carry):
        sels = [s_ref[n * span + t] >= thr for t in range(span)]
        for g in range(N_KV_HEADS):
            m_new, acc_new = _softmax_step_t(
                q_groups[g], k_ref[key_span(n), g * HEAD_DIM:(g + 1) * HEAD_DIM], vt_ref[n, g], sels,
                m_ref[g], acc_ref[g])
            m_ref[g] = m_new
            acc_ref[g] = acc_new
        return carry

    @pl.when(bounded)
    def _():
        lax.fori_loop(0, n_steps, bounded_body, 0)

    @pl.when(jnp.logical_not(bounded))
    def _():
        m_ref[...] = jnp.full(m_ref.shape, MASKED_LOGIT, F32)
        lax.fori_loop(0, n_steps, running_max_body, 0)

    for g in range(N_KV_HEADS):
        acc = acc_ref[g]
        out_t = acc[0:HEAD_DIM] * (1.0 / acc[HEAD_DIM:HEAD_DIM + 1])
        for r in range(GROUP):
            h = g * GROUP + r
            o_ref[:, h * HEAD_DIM:(h + 1) * HEAD_DIM] = out_t[:, r * tq:(r + 1) * tq].T


def _prompt_attention(qb, qib, wq, kb, vb, kib, kmx):
    t = qb.shape[0]
    tq = min(2 * LANES, t)
    assert t % tq == 0
    nt = t // tq
    span = 4 if nt % 4 == 0 else (2 if nt % 2 == 0 else 1)
    top_k = min(TOPK_MAX, t // 4)
    vt = vb.reshape(nt // span, span * tq, N_KV_HEADS, HEAD_DIM).transpose(0, 2, 3, 1)
    vt = jnp.concatenate([vt, jnp.ones((nt // span, N_KV_HEADS, SUBLANES, span * tq), BF16)], axis=2)
    row = lambda i: (i, 0)
    const = lambda i: (0, 0)
    return pl.pallas_call(
        functools.partial(_prompt_attn_kernel, top_k=top_k),
        grid=(nt,),
        in_specs=[
            pl.BlockSpec((tq, ATT_DIM), row),
            pl.BlockSpec((tq, QI_DIM), row),
            pl.BlockSpec((N_IDX_HEADS, tq), lambda i: (0, i)),
            pl.BlockSpec((t, KV_DIM), const),
            pl.BlockSpec(vt.shape, lambda i: (0, 0, 0, 0)),
            pl.BlockSpec((t, IDX_DIM), const),
            pl.BlockSpec((SUBLANES, LANES), const),
        ],
        out_specs=pl.BlockSpec((tq, ATT_DIM), row),
        out_shape=jax.ShapeDtypeStruct((t, ATT_DIM), F32),
        scratch_shapes=[
            pltpu.VMEM((nt, tq, tq), F32),
            pltpu.VMEM((nt, tq, tq), BF16),
            pltpu.VMEM((N_KV_HEADS, 1, GROUP * tq), F32),
            pltpu.VMEM((N_KV_HEADS, HEAD_DIM + SUBLANES, GROUP * tq), F32),
        ],
        compiler_params=pltpu.CompilerParams(
            dimension_semantics=("arbitrary",), vmem_limit_bytes=V7X_VMEM_LIMIT_BYTES),
        name="attn_prompt",
    )(qb, qib, wq.T, kb, vt, kib, kmx)


def _softmax_step(q, kt, vt, sel, m_old, l_old, acc_old):
    s = lax.dot_general(q, kt, _NT, preferred_element_type=F32)
    s = jnp.where(sel, s, MASKED_LOGIT)
    m_new = jnp.maximum(m_old, jnp.max(s, axis=-1, keepdims=True))
    p = jnp.exp2(s - m_new)
    alpha = jnp.exp2(m_old - m_new)
    l_new = alpha * l_old + jnp.sum(p, axis=-1, keepdims=True)
    acc_new = alpha * acc_old + jnp.dot(p.astype(BF16), vt, preferred_element_type=F32)
    return m_new, l_new, acc_new


def _sample_attn_kernel(pt_ref, qi_ref, wq_ref, q_ref, kin_ref, kn_ref, vn_ref,
                        ck_hbm, cv_hbm, ckit_hbm, o_ref,
                        kbuf, vbuf, kitbuf, s_main, s_tail, sem, *, layer, top_k, pages_per_chunk):
    b = pl.program_id(0)
    n_chunks, n_tok, chunk = s_main.shape
    page = chunk // pages_per_chunk

    def page_id(c, p):
        return pt_ref[b, c * pages_per_chunk + p]

    def kv_copies(hbm, buf, sem_row, c, slot):
        rows = page * N_KV_HEADS
        return [pltpu.make_async_copy(hbm.at[layer, page_id(c, p)],
                                      buf.at[slot, pl.ds(p * rows, rows)], sem.at[sem_row, slot])
                for p in range(pages_per_chunk)]

    def kit_copies(c, slot):
        return [pltpu.make_async_copy(ckit_hbm.at[layer, page_id(c, p)],
                                      kitbuf.at[slot, :, pl.ds(p * page, page)], sem.at[0, slot])
                for p in range(pages_per_chunk)]

    def start(copies):
        for cp in copies:
            cp.start()

    def wait(copies):
        for cp in copies:
            cp.wait()

    qi = qi_ref[0]
    wq = wq_ref[0]
    w_cols = [wq[:, h:h + 1] for h in range(N_IDX_HEADS)]

    def head_sum(s_all):
        acc = None
        for h in range(N_IDX_HEADS):
            term = jnp.maximum(s_all[h * n_tok:(h + 1) * n_tok, :], 0.0) * w_cols[h]
            acc = term if acc is None else acc + term
        return acc + 0.0

    start(kit_copies(0, 0))

    def score_body(c, carry):
        slot = c % 2

        @pl.when(c + 1 < n_chunks)
        def _():
            start(kit_copies(c + 1, 1 - slot))

        wait(kit_copies(c, slot))
        s_main[c] = head_sum(jnp.dot(qi, kitbuf[slot].astype(BF16), preferred_element_type=F32))
        return carry

    lax.fori_loop(0, n_chunks, score_body, 0)
    col = lax.broadcasted_iota(jnp.int32, (n_tok, LANES), 1)
    tok = lax.broadcasted_iota(jnp.int32, (n_tok, LANES), 0)
    s_new = head_sum(lax.dot_general(qi, kin_ref[0], _NT, preferred_element_type=F32))
    s_tail[0] = jnp.where(col <= tok, s_new, -jnp.inf)

    start(kv_copies(ck_hbm, kbuf, 1, 0, 0))
    start(kv_copies(cv_hbm, vbuf, 2, 0, 0))

    thr = _select_threshold([(s_main, n_chunks), (s_tail, 1)], (n_tok, 1), top_k, key_axis=1)

    rows = GROUP * n_tok
    thr_rows = jnp.concatenate([thr] * GROUP, axis=0)

    def attend(state, kv_of, s_tok):
        sel = jnp.concatenate([s_tok] * GROUP, axis=0) >= thr_rows
        new_state = []
        for g in range(N_KV_HEADS):
            kt, vt = kv_of(g)
            new_state.append(_softmax_step(q_ref[0, g], kt, vt, sel, *state[g]))
        return tuple(new_state)

    def attend_body(c, state):
        slot = c % 2

        @pl.when(c + 1 < n_chunks)
        def _():
            start(kv_copies(ck_hbm, kbuf, 1, c + 1, 1 - slot))
            start(kv_copies(cv_hbm, vbuf, 2, c + 1, 1 - slot))

        wait(kv_copies(ck_hbm, kbuf, 1, c, slot))
        wait(kv_copies(cv_hbm, vbuf, 2, c, slot))

        def kv_of(g):
            head_rows = pl.ds(g, chunk, stride=N_KV_HEADS)
            return kbuf[slot, head_rows, :].astype(BF16), vbuf[slot, head_rows, :].astype(BF16)

        return attend(state, kv_of, s_main[c])

    init = tuple((jnp.full((rows, 1), MASKED_LOGIT, F32), jnp.zeros((rows, 1), F32),
                  jnp.zeros((rows, HEAD_DIM), F32)) for _ in range(N_KV_HEADS))
    state = lax.fori_loop(0, n_chunks, attend_body, init)

    def new_kv(g):
        cols = slice(g * HEAD_DIM, (g + 1) * HEAD_DIM)
        return kn_ref[0, :, cols], vn_ref[0, :, cols]

    state = attend(state, new_kv, s_tail[0])
    for g in range(N_KV_HEADS):
        _, l_fin, acc_fin = state[g]
        o_ref[0, g] = acc_fin * (1.0 / l_fin)


def _sample_attention(qb, qib, wq, kb, vb, kib, cache_k_rows, cache_v_rows, cache_kidx_t, page_table, layer):
    nb, n_pages = page_table.shape
    n_tok = qb.shape[0] // nb
    page = cache_kidx_t.shape[3]
    assert n_tok == SUBLANES and n_tok <= LANES
    pages_per_chunk = 16
    while n_pages % pages_per_chunk:
        pages_per_chunk //= 2
    n_chunks = n_pages // pages_per_chunk
    chunk = pages_per_chunk * page
    top_k = min(TOPK_MAX, (n_pages * page + n_tok) // 4)

    q_s = qb.reshape(nb, n_tok, N_KV_HEADS, GROUP, HEAD_DIM).transpose(0, 2, 3, 1, 4)
    q_s = q_s.reshape(nb, N_KV_HEADS, GROUP * n_tok, HEAD_DIM)
    qi_s = qib.reshape(nb, n_tok, N_IDX_HEADS, IDX_DIM).transpose(0, 2, 1, 3)
    qi_s = qi_s.reshape(nb, N_IDX_HEADS * n_tok, IDX_DIM)
    wq_s = wq.reshape(nb, n_tok, N_IDX_HEADS)
    pad = lambda a: jnp.pad(a.reshape(nb, n_tok, a.shape[-1]), ((0, 0), (0, LANES - n_tok), (0, 0)))
    kin_s, kn_s, vn_s = pad(kib), pad(kb), pad(vb)

    seq3 = lambda b, pt: (b, 0, 0)
    seq4 = lambda b, pt: (b, 0, 0, 0)
    out = pl.pallas_call(
        functools.partial(_sample_attn_kernel, layer=layer, top_k=top_k,
                          pages_per_chunk=pages_per_chunk),
        grid_spec=pltpu.PrefetchScalarGridSpec(
            num_scalar_prefetch=1,
            grid=(nb,),
            in_specs=[
                pl.BlockSpec((1, N_IDX_HEADS * n_tok, IDX_DIM), seq3),
                pl.BlockSpec((1, n_tok, N_IDX_HEADS), seq3),
                pl.BlockSpec((1, N_KV_HEADS, GROUP * n_tok, HEAD_DIM), seq4),
                pl.BlockSpec((1, LANES, IDX_DIM), seq3),
                pl.BlockSpec((1, LANES, KV_DIM), seq3),
                pl.BlockSpec((1, LANES, KV_DIM), seq3),
                pl.BlockSpec(memory_space=pl.ANY),
                pl.BlockSpec(memory_space=pl.ANY),
                pl.BlockSpec(memory_space=pl.ANY),
            ],
            out_specs=pl.BlockSpec((1, N_KV_HEADS, GROUP * n_tok, HEAD_DIM), seq4),
            scratch_shapes=[
                pltpu.VMEM((2, chunk * N_KV_HEADS, HEAD_DIM), F32),
                pltpu.VMEM((2, chunk * N_KV_HEADS, HEAD_DIM), F32),
                pltpu.VMEM((2, IDX_DIM, chunk), F32),
                pltpu.VMEM((n_chunks, n_tok, chunk), F32),
                pltpu.VMEM((1, n_tok, LANES), F32),
                pltpu.SemaphoreType.DMA((3, 2)),
            ],
        ),
        out_shape=jax.ShapeDtypeStruct((nb, N_KV_HEADS, GROUP * n_tok, HEAD_DIM), F32),
        compiler_params=pltpu.CompilerParams(
            dimension_semantics=("arbitrary",), vmem_limit_bytes=V7X_VMEM_LIMIT_BYTES),
        name="attn_sample",
    )(page_table, qi_s, wq_s, q_s, kin_s, kn_s, vn_s, cache_k_rows, cache_v_rows, cache_kidx_t)
    out = out.reshape(nb, N_KV_HEADS, GROUP, n_tok, HEAD_DIM).transpose(0, 3, 1, 2, 4)
    return out.reshape(nb * n_tok, ATT_DIM)


def _pack_w_in(w):
    ends, off = [], 0
    for size in (CONV_DIM, CONV_DIM, CONV_DIM, CONV_DIM, ATT_DIM, KV_DIM, KV_DIM, QI_DIM, IDX_DIM,
                 N_IDX_HEADS, ATT_DIM, D_MODEL, D_MODEL):
        ends.append((off, off + size))
        off += size
    (uh, ub, uc, za, q, k, v, qi, ki, wi, zb, ga, gb) = [w[:, a:b] for a, b in ends]
    pad = jnp.zeros((w.shape[0], LANES - IDX_DIM - N_IDX_HEADS), w.dtype)
    return jnp.concatenate([uh, ub, uc, za, q, k, v, qi, zb, ga, gb, ki, wi, pad], axis=1).astype(BF16)


def kernel(x_prompt, x_sample, c_prompt, c_sample, cache_k, cache_v, cache_kidx, state_conv, page_table,
           norm_g, w_mod, b_mod, w_in, conv_w, q_norm, k_norm, w_proj_a, w_proj_b, w_out):
    depth = w_in.shape[0]
    bp, seq, _ = x_prompt.shape
    nb, n_tok, _ = x_sample.shape
    assert bp == 1
    n_phys, page = cache_k.shape[1], cache_k.shape[2]
    cache_k = cache_k.reshape(depth, n_phys, page * N_KV_HEADS, HEAD_DIM)
    cache_v = cache_v.reshape(depth, n_phys, page * N_KV_HEADS, HEAD_DIM)
    cache_kidx = jnp.swapaxes(cache_kidx, 2, 3)

    mod = _modulation(jnp.concatenate([c_prompt, c_sample], axis=0), w_mod, b_mod)

    yp = x_prompt.reshape(seq, D_MODEL)
    ys = x_sample.reshape(nb * n_tok, D_MODEL)
    outs = [[] for _ in range(8)]
    for l in range(depth):
        w_in_p = _pack_w_in(w_in[l])
        w_pa, w_pb, w_o = (w_proj_a[l].astype(BF16), w_proj_b[l].astype(BF16), w_out[l].astype(BF16))
        ng, qn, kn = norm_g[l][None, :], q_norm[l][None, :], k_norm[l][None, :]

        shift, scale, gate = jnp.split(mod[l, 0:bp], 3, axis=-1)
        (ap, qb, kf, kb, vf, vb, qib, kif, kib, wq, zb, sgb, ust, kmx) = _projection(
            yp, ng, scale, shift, w_in_p, conv_w[l], qn, kn, w_pa)
        attn = _prompt_attention(qb, qib, wq, kb, vb, kib, kmx)
        yp = _output(ap, attn, zb, sgb, yp, gate, w_pb, w_o)
        outs[0].append(kf.reshape(bp, seq, N_KV_HEADS, HEAD_DIM))
        outs[1].append(vf.reshape(bp, seq, N_KV_HEADS, HEAD_DIM))
        outs[2].append(kif.reshape(bp, seq, IDX_DIM))
        outs[3].append(ust[SUBLANES - (CONV_WIDTH - 1):].reshape(bp, CONV_WIDTH - 1, CONV_DIM))

        shift, scale, gate = [jnp.repeat(m, n_tok, axis=0) for m in jnp.split(mod[l, bp:], 3, axis=-1)]
        st = state_conv[l]
        zeros = jnp.zeros((nb, n_tok - 2, CONV_DIM), F32)
        p1 = jnp.concatenate([st[:, 1:2], st[:, 1:2], zeros], axis=1).reshape(nb * n_tok, CONV_DIM)
        p2 = jnp.concatenate([st[:, 0:1], st[:, 1:2], zeros], axis=1).reshape(nb * n_tok, CONV_DIM)
        (ap, qb, kf, kb, vf, vb, qib, kif, kib, wq, zb, sgb, ust, _) = _projection(
            ys, ng, scale, shift, w_in_p, conv_w[l], qn, kn, w_pa, prev=(p1, p2), seg_len=n_tok)
        attn = _sample_attention(qb, qib, wq, kb, vb, kib, cache_k, cache_v, cache_kidx, page_table, l)
        ys = _output(ap, attn, zb, sgb, ys, gate, w_pb, w_o)
        outs[4].append(kf.reshape(nb, n_tok, N_KV_HEADS, HEAD_DIM))
        outs[5].append(vf.reshape(nb, n_tok, N_KV_HEADS, HEAD_DIM))
        outs[6].append(kif.reshape(nb, n_tok, IDX_DIM))
        outs[7].append(ust.reshape(nb, n_tok, CONV_DIM)[:, n_tok - (CONV_WIDTH - 1):])

    return (yp.reshape(bp, seq, D_MODEL), ys.reshape(nb, n_tok, D_MODEL),
            *[jnp.stack(o) for o in outs])
```

```python
import functools

import jax
import jax.numpy as jnp
from jax import lax
from jax.experimental import pallas as pl
from jax.experimental.pallas import tpu as pltpu

F32 = jnp.float32
BF16 = jnp.bfloat16

D_MODEL = 1024
CONV_DIM = 1024
CONV_WIDTH = 3
N_HEADS = 8
N_KV_HEADS = 2
HEAD_DIM = 128
ATT_DIM = N_HEADS * HEAD_DIM
KV_DIM = N_KV_HEADS * HEAD_DIM
GROUP = N_HEADS // N_KV_HEADS
N_IDX_HEADS = 8
IDX_DIM = 64
QI_DIM = N_IDX_HEADS * IDX_DIM
TOPK_MAX = 256
EPS = 1e-6
MASKED_LOGIT = -1e30
LOG2_E = 1.4426950408889634
MAX_STATIC_BOUND = 50.0
BOUND_PAD = 1.02

LANES = 128
SUBLANES = 8
V7X_VMEM_LIMIT_BYTES = 56 * 1024 * 1024

OFF_UH = 0
OFF_UB = OFF_UH + CONV_DIM
OFF_UC = OFF_UB + CONV_DIM
OFF_ZA = OFF_UC + CONV_DIM
OFF_Q = OFF_ZA + CONV_DIM
OFF_K = OFF_Q + ATT_DIM
OFF_V = OFF_K + KV_DIM
OFF_QI = OFF_V + KV_DIM
OFF_ZB = OFF_QI + QI_DIM
OFF_GA = OFF_ZB + ATT_DIM
OFF_GB = OFF_GA + D_MODEL
OFF_KW = OFF_GB + D_MODEL
P_COLS = OFF_KW + LANES

KEY_LOWEST_FINITE = -2139095040
INT_MIN = -(2 ** 31)

_NT = (((1,), (1,)), ((), ()))


def _silu(x):
    return x * jax.nn.sigmoid(x)


def _rms_rows(x, g):
    ms = jnp.mean(x * x, axis=-1, keepdims=True)
    return (x * lax.rsqrt(ms + EPS)) * g


def _mod_kernel(c_ref, w_ref, b_ref, o_ref):
    s = _silu(c_ref[...]).astype(BF16)
    o_ref[0] = jnp.dot(s, w_ref[0].astype(BF16), preferred_element_type=F32) + b_ref[0]


def _modulation(c_all, w_mod, b_mod):
    depth = w_mod.shape[0]
    nb = c_all.shape[0]
    return pl.pallas_call(
        _mod_kernel,
        grid=(depth,),
        in_specs=[
            pl.BlockSpec((nb, D_MODEL), lambda l: (0, 0)),
            pl.BlockSpec((1, D_MODEL, 3 * D_MODEL), lambda l: (l, 0, 0)),
            pl.BlockSpec((1, 1, 3 * D_MODEL), lambda l: (l, 0, 0)),
        ],
        out_specs=pl.BlockSpec((1, nb, 3 * D_MODEL), lambda l: (l, 0, 0)),
        out_shape=jax.ShapeDtypeStruct((depth, nb, 3 * D_MODEL), F32),
        compiler_params=pltpu.CompilerParams(
            dimension_semantics=("arbitrary",), vmem_limit_bytes=V7X_VMEM_LIMIT_BYTES),
        name="adaln_mod",
    )(c_all, w_mod, b_mod.reshape(depth, 1, 3 * D_MODEL))


def _proj_kernel(*refs, seg_len):
    if seg_len is None:
        (x_ref, ng_ref, sc_ref, sh_ref, w_ref, cw_ref, qn_ref, kn_ref, wpa_ref,
         ap_ref, qb_ref, kf_ref, kb_ref, vf_ref, vb_ref, qib_ref, kif_ref, kib_ref, wq_ref,
         zb_ref, sgb_ref, ust_ref, kmx_ref, carry_ref) = refs
    else:
        (x_ref, ng_ref, sc_ref, sh_ref, w_ref, cw_ref, qn_ref, kn_ref, wpa_ref, p1_ref, p2_ref,
         ap_ref, qb_ref, kf_ref, kb_ref, vf_ref, vb_ref, qib_ref, kif_ref, kib_ref, wq_ref,
         zb_ref, sgb_ref, ust_ref, kmx_ref) = refs
    tm = x_ref.shape[0]

    h = _rms_rows(x_ref[...], ng_ref[...]) * (1.0 + sc_ref[...]) + sh_ref[...]
    hb = h.astype(BF16)

    def proj(off, width):
        return jnp.dot(hb, w_ref[:, off:off + width], preferred_element_type=F32)

    u = proj(OFF_UC, CONV_DIM) * proj(OFF_UH, CONV_DIM)
    row = lax.broadcasted_iota(jnp.int32, (tm, 1), 0)
    if seg_len is None:
        @pl.when(pl.program_id(0) == 0)
        def _():
            carry_ref[...] = jnp.zeros_like(carry_ref)
        prev2 = carry_ref[SUBLANES - 2:SUBLANES - 1, :]
        prev1 = carry_ref[SUBLANES - 1:SUBLANES, :]
        pos = row
        p1 = prev1
        p2 = jnp.where(pos == 0, prev2, prev1)
    else:
        pos = row % seg_len
        p1 = p1_ref[...]
        p2 = p2_ref[...]
    u1 = jnp.where(pos == 0, p1, pltpu.roll(u, 1, 0))
    u2 = jnp.where(pos < 2, p2, pltpu.roll(u, 2, 0))
    conv = (cw_ref[0:1, :] * u2 + cw_ref[1:2, :] * u1) + cw_ref[2:3, :] * u
    if seg_len is None:
        carry_ref[...] = u[tm - SUBLANES:tm, :]
        ust_ref[...] = u[tm - SUBLANES:tm, :]
    else:
        ust_ref[...] = u
    branch_a = (proj(OFF_UB, CONV_DIM) * conv) * _silu(proj(OFF_ZA, CONV_DIM))
    ap_ref[...] = jax.nn.sigmoid(proj(OFF_GA, D_MODEL)) * jnp.dot(
        branch_a.astype(BF16), wpa_ref[...], preferred_element_type=F32)

    qf = proj(OFF_Q, ATT_DIM)
    for hd in range(N_HEADS):
        sl = slice(hd * HEAD_DIM, (hd + 1) * HEAD_DIM)
        qb_ref[:, sl] = (_rms_rows(qf[:, sl], qn_ref[...]) * (HEAD_DIM ** -0.5 * LOG2_E)).astype(BF16)
    kf = proj(OFF_K, KV_DIM)

    @pl.when(pl.program_id(0) == 0)
    def _():
        kmx_ref[...] = jnp.zeros_like(kmx_ref)

    for g in range(N_KV_HEADS):
        sl = slice(g * HEAD_DIM, (g + 1) * HEAD_DIM)
        kn = _rms_rows(kf[:, sl], kn_ref[...])
        kf_ref[:, sl] = kn
        kb_ref[:, sl] = kn.astype(BF16)
        kr = kn.astype(BF16).astype(F32)
        n2 = jnp.max(jnp.sum(kr * kr, axis=-1, keepdims=True), axis=0, keepdims=True)
        kmx_ref[g:g + 1, :] = jnp.maximum(kmx_ref[g:g + 1, :], n2)
    vf = proj(OFF_V, KV_DIM)
    vf_ref[...] = vf
    vb_ref[...] = vf.astype(BF16)
    qib_ref[...] = proj(OFF_QI, QI_DIM).astype(BF16)
    kw = proj(OFF_KW, LANES)
    kif_ref[...] = kw[:, 0:IDX_DIM]
    kib_ref[...] = kw[:, 0:IDX_DIM].astype(BF16)
    wq_ref[...] = (kw[:, IDX_DIM:IDX_DIM + N_IDX_HEADS] * (N_IDX_HEADS ** -0.5)) * (IDX_DIM ** -0.5)
    zb_ref[...] = _silu(proj(OFF_ZB, ATT_DIM))
    sgb_ref[...] = jax.nn.sigmoid(proj(OFF_GB, D_MODEL))


def _row_tile(t):
    for tm in (512, 256, 128, 64, 32, 16, 8):
        if t % tm == 0:
            return tm
    raise ValueError(f"row count {t} is not a multiple of {SUBLANES}")


def _projection(x, ng, scale, shift, w_in_p, conv_w, qn, kn, w_pa, prev=None, seg_len=None):
    t = x.shape[0]
    tm = _row_tile(t) if seg_len is None else t
    gr = scale.shape[0]
    assert gr in (1, t)
    row = lambda i: (i, 0)
    const = lambda i: (0, 0)
    mod_spec = pl.BlockSpec((1 if gr == 1 else tm, D_MODEL), const if gr == 1 else row)
    in_specs = [
        pl.BlockSpec((tm, D_MODEL), row),
        pl.BlockSpec((1, D_MODEL), const),
        mod_spec, mod_spec,
        pl.BlockSpec((D_MODEL, P_COLS), const),
        pl.BlockSpec((CONV_WIDTH, CONV_DIM), const),
        pl.BlockSpec((1, HEAD_DIM), const),
        pl.BlockSpec((1, HEAD_DIM), const),
        pl.BlockSpec((CONV_DIM, D_MODEL), const),
    ]
    args = [x, ng, scale, shift, w_in_p, conv_w, qn, kn, w_pa]
    scratch = []
    if seg_len is None:
        ust_shape, ust_spec = (SUBLANES, CONV_DIM), pl.BlockSpec((SUBLANES, CONV_DIM), const)
        scratch.append(pltpu.VMEM((SUBLANES, CONV_DIM), F32))
    else:
        in_specs += [pl.BlockSpec((tm, CONV_DIM), row)] * 2
        args += list(prev)
        ust_shape, ust_spec = (t, CONV_DIM), pl.BlockSpec((tm, CONV_DIM), row)
    widths = [(D_MODEL, F32), (ATT_DIM, BF16), (KV_DIM, F32), (KV_DIM, BF16), (KV_DIM, F32),
              (KV_DIM, BF16), (QI_DIM, BF16), (IDX_DIM, F32), (IDX_DIM, BF16), (N_IDX_HEADS, F32),
              (ATT_DIM, F32), (D_MODEL, F32)]
    out_shape = [jax.ShapeDtypeStruct((t, w), dt) for w, dt in widths]
    out_specs = [pl.BlockSpec((tm, w), row) for w, _ in widths]
    out_shape.append(jax.ShapeDtypeStruct(ust_shape, F32))
    out_specs.append(ust_spec)
    out_shape.append(jax.ShapeDtypeStruct((SUBLANES, LANES), F32))
    out_specs.append(pl.BlockSpec((SUBLANES, LANES), const))
    return pl.pallas_call(
        functools.partial(_proj_kernel, seg_len=seg_len),
        grid=(t // tm,),
        in_specs=in_specs,
        out_specs=out_specs,
        out_shape=out_shape,
        scratch_shapes=scratch,
        compiler_params=pltpu.CompilerParams(
            dimension_semantics=("arbitrary",), vmem_limit_bytes=V7X_VMEM_LIMIT_BYTES),
        name="proj_prompt" if seg_len is None else "proj_sample",
    )(*args)


def _out_kernel(ap_ref, at_ref, zb_ref, sgb_ref, x_ref, gate_ref, wpb_ref, wout_ref, y_ref):
    branch_b = (at_ref[...] * zb_ref[...]).astype(BF16)
    merged = ap_ref[...] + sgb_ref[...] * jnp.dot(branch_b, wpb_ref[...], preferred_element_type=F32)
    y_ref[...] = x_ref[...] + gate_ref[...] * jnp.dot(
        merged.astype(BF16), wout_ref[...], preferred_element_type=F32)


def _output(a_part, attn, zb, sgb, x, gate, w_pb, w_out):
    t = x.shape[0]
    tm = _row_tile(t)
    gr = gate.shape[0]
    assert gr in (1, t)
    row = lambda i: (i, 0)
    const = lambda i: (0, 0)
    act = pl.BlockSpec((tm, D_MODEL), row)
    return pl.pallas_call(
        _out_kernel,
        grid=(t // tm,),
        in_specs=[act, act, act, act, act,
                  pl.BlockSpec((1 if gr == 1 else tm, D_MODEL), const if gr == 1 else row),
                  pl.BlockSpec((ATT_DIM, D_MODEL), const),
                  pl.BlockSpec((D_MODEL, D_MODEL), const)],
        out_specs=act,
        out_shape=jax.ShapeDtypeStruct((t, D_MODEL), F32),
        compiler_params=pltpu.CompilerParams(
            dimension_semantics=("arbitrary",), vmem_limit_bytes=V7X_VMEM_LIMIT_BYTES),
        name="out_proj",
    )(a_part, attn, zb, sgb, x, gate, w_pb, w_out)


def _key_to_float(key):
    bits = jnp.where(key >= 0, key, key ^ jnp.int32(0x7FFFFFFF))
    return lax.bitcast_convert_type(bits, F32)


def _tree_sum(parts):
    while len(parts) > 1:
        parts = [parts[i] + parts[i + 1] for i in range(0, len(parts) - 1, 2)] + (
            [parts[-1]] if len(parts) % 2 else [])
    return parts[0]


def _fold_keys(x, key_axis, rows=SUBLANES):
    if key_axis == 0:
        return _tree_sum([x[c * rows:(c + 1) * rows] for c in range(x.shape[0] // rows)])
    return _tree_sum([x[:, c * LANES:(c + 1) * LANES] for c in range(x.shape[1] // LANES)])


COUNT_UNROLL = 4


def _loop_tiles(n_tiles, body, init):
    if isinstance(n_tiles, int) and n_tiles <= 16:
        for j in range(n_tiles):
            init = body(j, init)
        return init

    def group_body(n, carry):
        for t in range(COUNT_UNROLL):
            carry = body(n * COUNT_UNROLL + t, carry)
        return carry

    n_groups = n_tiles // COUNT_UNROLL
    init = lax.fori_loop(0, n_groups, group_body, init)
    return lax.fori_loop(n_groups * COUNT_UNROLL, n_tiles, body, init)


def _count(segments, pred, stat_shape, key_axis):
    fold_shape = (SUBLANES, stat_shape[1]) if key_axis == 0 else (stat_shape[0], LANES)
    total = jnp.zeros(fold_shape, F32)
    for ref, n_tiles in segments:
        def body(j, cnt, ref=ref):
            return cnt + _fold_keys(jnp.where(pred(ref[j]), 1.0, 0.0), key_axis)
        total = _loop_tiles(n_tiles, body, total)
    return jnp.sum(total, axis=key_axis, keepdims=True)


def _truncate_to_bf16(x):
    bits = lax.bitcast_convert_type(x, jnp.int32) & jnp.int32(-65536)
    return lax.bitcast_convert_type(bits, F32).astype(BF16)


def _count_coarse(coarse, thr):
    pack = 2 * SUBLANES
    total = jnp.zeros((pack, thr.shape[1]), F32)
    one, zero = jnp.ones((), thr.dtype), jnp.zeros((), thr.dtype)
    for ref, n_tiles in coarse:
        def body(j, cnt, ref=ref):
            hit = jnp.where(ref[j] >= thr, one, zero)
            return cnt + _fold_keys(hit, 0, rows=pack).astype(F32)
        total = _loop_tiles(n_tiles, body, total)
    return jnp.sum(total, axis=0, keepdims=True)


def _merge_two_smallest(a, b):
    return jnp.minimum(a[0], b[0]), jnp.minimum(jnp.maximum(a[0], b[0]), jnp.minimum(a[1], b[1]))


def _two_smallest(segments, thr, key_axis):
    fold_shape = (SUBLANES, thr.shape[1]) if key_axis == 0 else (thr.shape[0], LANES)
    best = (jnp.full(fold_shape, jnp.inf, F32), jnp.full(fold_shape, jnp.inf, F32))
    for ref, n_tiles in segments:
        def body(j, carry, ref=ref):
            tile = ref[j]
            x = jnp.where(tile >= thr, tile, jnp.inf)
            if key_axis == 0:
                parts = [x[c * SUBLANES:(c + 1) * SUBLANES] for c in range(x.shape[0] // SUBLANES)]
            else:
                parts = [x[:, c * LANES:(c + 1) * LANES] for c in range(x.shape[1] // LANES)]
            if len(parts) % 2:
                parts.append(jnp.full(fold_shape, jnp.inf, F32))
            pairs = [(jnp.minimum(parts[c], parts[c + 1]), jnp.maximum(parts[c], parts[c + 1]))
                     for c in range(0, len(parts), 2)]
            while len(pairs) > 1:
                pairs = [_merge_two_smallest(pairs[c], pairs[c + 1]) for c in range(0, len(pairs) - 1, 2)] + (
                    [pairs[-1]] if len(pairs) % 2 else [])
            return _merge_two_smallest(carry, pairs[0])
        best = _loop_tiles(n_tiles, body, best)
    first, second = best
    low1 = jnp.min(first, axis=key_axis, keepdims=True)
    at_low1 = first == low1
    n_low1 = jnp.sum(jnp.where(at_low1, 1.0, 0.0), axis=key_axis, keepdims=True)
    others = jnp.min(jnp.where(at_low1, second, first), axis=key_axis, keepdims=True)
    return low1, jnp.where(n_low1 >= 2.0, low1, others)


def _kth_largest(segments, stat_shape, k, key_axis, coarse=None):
    kf = float(k)

    def step(i, state, count_fn):
        prefix, cge = state
        cand = prefix ^ jnp.left_shift(jnp.int32(1), 31 - i)
        cnt = count_fn(_key_to_float(cand))
        take = cnt >= kf
        return jnp.where(take, cand, prefix), jnp.where(take, cnt, cge)

    def count_fine(t):
        return _count(segments, lambda tile: tile >= t, stat_shape, key_axis)

    state = (jnp.full(stat_shape, INT_MIN, jnp.int32), jnp.full(stat_shape, jnp.inf, F32))
    first_fine = 0
    if coarse is not None:
        assert key_axis == 0
        first_fine = 16
        state = lax.fori_loop(
            0, first_fine,
            lambda i, st: step(i, st, lambda t: _count_coarse(coarse, _truncate_to_bf16(t))), state)

    def cond(carry):
        i, _, cge = carry
        return (i < 32) & (jnp.max(cge) > kf + 1.0)

    def body(carry):
        i, prefix, cge = carry
        return (i + 2,) + step(i + 1, step(i, (prefix, cge), count_fine), count_fine)

    _, prefix, cge = lax.while_loop(cond, body, (jnp.int32(first_fine),) + state)
    thr = _key_to_float(jnp.maximum(prefix, jnp.int32(KEY_LOWEST_FINITE)))
    cge = jnp.where(cge == jnp.inf, 0.0, cge)

    low1, low2 = _two_smallest(segments, thr, key_axis)
    one_over = cge == kf + 1.0
    distinct = one_over & (low2 > low1)
    thr = jnp.where(one_over, jnp.where(distinct, low2, low1), thr)
    return thr, jnp.where(distinct, kf, cge)


def _drop_excess_ties(segments, thr, cge, k, key_axis):
    excess = cge > float(k)
    n_eq = _count(segments, lambda tile: tile == thr, thr.shape, key_axis)
    need = float(k) - (cge - n_eq)
    seen = jnp.zeros(thr.shape, F32)
    for ref, n_tiles in segments:
        width = ref.shape[1 + key_axis]
        cw = min(width, 2 * LANES)
        r_i = lax.broadcasted_iota(jnp.int32, (cw, cw), 0)
        c_i = lax.broadcasted_iota(jnp.int32, (cw, cw), 1)
        tri = jnp.where(r_i >= c_i if key_axis == 0 else r_i <= c_i, 1.0, 0.0).astype(BF16)

        def drop_body(j, seen, ref=ref, width=width, cw=cw, tri=tri):
            for c in range(width // cw):
                idx = (j, slice(c * cw, (c + 1) * cw), slice(None)) if key_axis == 0 else (
                    j, slice(None), slice(c * cw, (c + 1) * cw))
                tile = ref[idx]
                eq = tile == thr
                eq_b = jnp.where(eq, 1.0, 0.0).astype(BF16)
                within = (jnp.dot(tri, eq_b, preferred_element_type=F32) if key_axis == 0
                          else jnp.dot(eq_b, tri, preferred_element_type=F32))
                drop = eq & excess & (seen + within > need)
                ref[idx] = jnp.where(drop, -jnp.inf, tile)
                seen = seen + jnp.sum(jnp.where(eq, 1.0, 0.0), axis=key_axis, keepdims=True)
            return seen

        seen = lax.fori_loop(0, n_tiles, drop_body, seen)


def _select_threshold(segments, stat_shape, k, key_axis, coarse=None):
    thr, cge = _kth_largest(segments, stat_shape, k, key_axis, coarse)

    @pl.when(jnp.max(cge) > float(k))
    def _():
        _drop_excess_ties(segments, thr, cge, k, key_axis)

    return thr


def _softmax_step_t(q, kt, vt_aug, sels, m_old, acc_old):
    width = sels[0].shape[1]
    rows = sels[0].shape[0]
    s = lax.dot_general(kt, q, _NT, preferred_element_type=F32)
    s = jnp.concatenate(
        [jnp.concatenate([jnp.where(sel, s[t * rows:(t + 1) * rows, r * width:(r + 1) * width], MASKED_LOGIT)
                          for r in range(q.shape[0] // width)], axis=1)
         for t, sel in enumerate(sels)], axis=0)
    m_new = jnp.maximum(m_old, jnp.max(s, axis=0, keepdims=True))
    p = jnp.exp2(s - m_new)
    alpha = jnp.exp2(m_old - m_new)
    acc_new = alpha * acc_old + jnp.dot(vt_aug, p.astype(BF16), preferred_element_type=F32)
    return m_new, acc_new


def _prompt_attn_kernel(q_ref, qi_ref, wqt_ref, k_ref, vt_ref, ki_ref, kmx_ref, o_ref,
                        s_ref, c_ref, m_ref, acc_ref, *, top_k):
    tq = q_ref.shape[0]
    tk = s_ref.shape[1]
    i = pl.program_id(0)

    qi = qi_ref[...]
    wqt = wqt_ref[...]
    qi_all = jnp.concatenate([qi[:, h * IDX_DIM:(h + 1) * IDX_DIM] for h in range(N_IDX_HEADS)], axis=0)
    w_all = jnp.concatenate([wqt[h:h + 1, :] for h in range(N_IDX_HEADS)], axis=1)

    def scores(kt):
        terms = jnp.maximum(lax.dot_general(kt, qi_all, _NT, preferred_element_type=F32), 0.0) * w_all
        acc = terms[:, 0:tq]
        for h in range(1, N_IDX_HEADS):
            acc = acc + terms[:, h * tq:(h + 1) * tq]
        return acc + 0.0

    def key_rows(j):
        return pl.ds(pl.multiple_of(j * tk, tk), tk)

    def put_scores(j, sc):
        s_ref[j] = sc
        c_ref[j] = _truncate_to_bf16(sc)

    span = vt_ref.shape[3] // tk

    def score_span_body(n, carry):
        rows = pl.ds(pl.multiple_of(n * (span * tk), span * tk), span * tk)
        sc = scores(ki_ref[rows, :])
        for t in range(span):
            put_scores(n * span + t, sc[t * tk:(t + 1) * tk])
        return carry

    def score_body(j, carry):
        put_scores(j, scores(ki_ref[key_rows(j), :]))
        return carry

    lax.fori_loop(0, i // span, score_span_body, 0)
    lax.fori_loop((i // span) * span, i, score_body, 0)
    visible = (lax.broadcasted_iota(jnp.int32, (tk, tq), 0)
               <= lax.broadcasted_iota(jnp.int32, (tk, tq), 1))
    put_scores(i, jnp.where(visible, scores(ki_ref[key_rows(i), :]), -jnp.inf))

    thr = _select_threshold([(s_ref, i + 1)], (1, tq), top_k, key_axis=0, coarse=[(c_ref, i + 1)])

    n_steps = (i + span) // span
    for extra in range(1, span):
        @pl.when(i + extra < n_steps * span)
        def _():
            s_ref[i + extra] = jnp.full((tk, tq), -jnp.inf, F32)
    acc_ref[...] = jnp.zeros(acc_ref.shape, F32)

    q_groups = [jnp.concatenate([q_ref[:, h * HEAD_DIM:(h + 1) * HEAD_DIM]
                                 for h in range(g * GROUP, (g + 1) * GROUP)], axis=0)
                for g in range(N_KV_HEADS)]

    ones = jnp.ones((SUBLANES, HEAD_DIM), BF16)

    def sq_norms(h):
        qf = q_ref[:, h * HEAD_DIM:(h + 1) * HEAD_DIM].astype(F32)
        return lax.dot_general(ones, (qf * qf).astype(BF16), _NT, preferred_element_type=F32)[0:1]

    bounds = [jnp.sqrt(jnp.concatenate([sq_norms(h) for h in range(g * GROUP, (g + 1) * GROUP)], axis=1)
                       * kmx_ref[g:g + 1, 0:1]) * BOUND_PAD
              for g in range(N_KV_HEADS)]
    bounded = jnp.max(jnp.maximum(bounds[0], bounds[1])) <= MAX_STATIC_BOUND

    def key_span(n):
        return pl.ds(pl.multiple_of(n * (span * tk), span * tk), span * tk)

    def bounded_body(n, carry):
        masks = [jnp.where(s_ref[n * span + t] >= thr, 1.0, 0.0).astype(BF16) for t in range(span)]
        mask = jnp.concatenate([jnp.concatenate([mk] * GROUP, axis=1) for mk in masks], axis=0)
        for g in range(N_KV_HEADS):
            s = lax.dot_general(k_ref[key_span(n), g * HEAD_DIM:(g + 1) * HEAD_DIM], q_groups[g], _NT,
                                preferred_element_type=F32)
            p = jnp.exp2(s - bounds[g]).astype(BF16) * mask
            acc_ref[g] = acc_ref[g] + jnp.dot(vt_ref[n, g], p, preferred_element_type=F32)
        return carry

    def running_max_body(n, carry):
        sels = [s_ref[n * span + t] >= thr for t in range(span)]
        for g in range(N_KV_HEADS):
            m_new, acc_new = _softmax_step_t(
                q_groups[g], k_ref[key_span(n), g * HEAD_DIM:(g + 1) * HEAD_DIM], vt_ref[n, g], sels,
                m_ref[g], acc_ref[g])
            m_ref[g] = m_new
            acc_ref[g] = acc_new
        return carry

    @pl.when(bounded)
    def _():
        lax.fori_loop(0, n_steps, bounded_body, 0)

    @pl.when(jnp.logical_not(bounded))
    def _():
        m_ref[...] = jnp.full(m_ref.shape, MASKED_LOGIT, F32)
        lax.fori_loop(0, n_steps, running_max_body, 0)

    for g in range(N_KV_HEADS):
        acc = acc_ref[g]
        out_t = acc[0:HEAD_DIM] * (1.0 / acc[HEAD_DIM:HEAD_DIM + 1])
        for r in range(GROUP):
            h = g * GROUP + r
            o_ref[:, h * HEAD_DIM:(h + 1) * HEAD_DIM] = out_t[:, r * tq:(r + 1) * tq].T


def _prompt_attention(qb, qib, wq, kb, vb, kib, kmx):
    t = qb.shape[0]
    tq = min(2 * LANES, t)
    assert t % tq == 0
    nt = t // tq
    span = 4 if nt % 4 == 0 else (2 if nt % 2 == 0 else 1)
    top_k = min(TOPK_MAX, t // 4)
    vt = vb.reshape(nt // span, span * tq, N_KV_HEADS, HEAD_DIM).transpose(0, 2, 3, 1)
    vt = jnp.concatenate([vt, jnp.ones((nt // span, N_KV_HEADS, SUBLANES, span * tq), BF16)], axis=2)
    row = lambda i: (i, 0)
    const = lambda i: (0, 0)
    return pl.pallas_call(
        functools.partial(_prompt_attn_kernel, top_k=top_k),
        grid=(nt,),
        in_specs=[
            pl.BlockSpec((tq, ATT_DIM), row),
            pl.BlockSpec((tq, QI_DIM), row),
            pl.BlockSpec((N_IDX_HEADS, tq), lambda i: (0, i)),
            pl.BlockSpec((t, KV_DIM), const),
            pl.BlockSpec(vt.shape, lambda i: (0, 0, 0, 0)),
            pl.BlockSpec((t, IDX_DIM), const),
            pl.BlockSpec((SUBLANES, LANES), const),
        ],
        out_specs=pl.BlockSpec((tq, ATT_DIM), row),
        out_shape=jax.ShapeDtypeStruct((t, ATT_DIM), F32),
        scratch_shapes=[
            pltpu.VMEM((nt, tq, tq), F32),
            pltpu.VMEM((nt, tq, tq), BF16),
            pltpu.VMEM((N_KV_HEADS, 1, GROUP * tq), F32),
            pltpu.VMEM((N_KV_HEADS, HEAD_DIM + SUBLANES, GROUP * tq), F32),
        ],
        compiler_params=pltpu.CompilerParams(
            dimension_semantics=("arbitrary",), vmem_limit_bytes=V7X_VMEM_LIMIT_BYTES),
        name="attn_prompt",
    )(qb, qib, wq.T, kb, vt, kib, kmx)


def _softmax_step(q, kt, vt, sel, m_old, l_old, acc_old):
    s = lax.dot_general(q, kt, _NT, preferred_element_type=F32)
    s = jnp.where(sel, s, MASKED_LOGIT)
    m_new = jnp.maximum(m_old, jnp.max(s, axis=-1, keepdims=True))
    p = jnp.exp2(s - m_new)
    alpha = jnp.exp2(m_old - m_new)
    l_new = alpha * l_old + jnp.sum(p, axis=-1, keepdims=True)
    acc_new = alpha * acc_old + jnp.dot(p.astype(BF16), vt, preferred_element_type=F32)
    return m_new, l_new, acc_new


KV_SLOTS = 4


def _sample_attn_kernel(pt_ref, qi_ref, wq_ref, q_ref, kin_ref, kn_ref, vn_ref,
                        ck_hbm, cv_hbm, ckit_hbm, o_ref,
                        kbuf, vbuf, kitbuf, s_main, s_tail, sem, *, layer, top_k, pages_per_chunk):
    b = pl.program_id(0)
    n_chunks, n_tok, chunk = s_main.shape
    page = chunk // pages_per_chunk

    def page_id(c, p):
        return pt_ref[b, c * pages_per_chunk + p]

    def kv_copies(hbm, buf, sem_row, c, slot):
        rows = page * N_KV_HEADS
        return [pltpu.make_async_copy(hbm.at[layer, page_id(c, p)],
                                      buf.at[slot, pl.ds(p * rows, rows)], sem.at[sem_row, slot])
                for p in range(pages_per_chunk)]

    def kit_copies(c, slot):
        return [pltpu.make_async_copy(ckit_hbm.at[layer, page_id(c, p)],
                                      kitbuf.at[slot, :, pl.ds(p * page, page)], sem.at[0, slot])
                for p in range(pages_per_chunk)]

    def start(copies):
        for cp in copies:
            cp.start()

    def wait(copies):
        for cp in copies:
            cp.wait()

    qi = qi_ref[0]
    wq = wq_ref[0]
    w_cols = [wq[:, h:h + 1] for h in range(N_IDX_HEADS)]

    def head_sum(s_all):
        acc = None
        for h in range(N_IDX_HEADS):
            term = jnp.maximum(s_all[h * n_tok:(h + 1) * n_tok, :], 0.0) * w_cols[h]
            acc = term if acc is None else acc + term
        return acc + 0.0

    start(kit_copies(0, 0))
    kv_slots = kbuf.shape[0]
    for c0 in range(min(kv_slots - 1, n_chunks)):
        start(kv_copies(ck_hbm, kbuf, 1, c0, c0))
        start(kv_copies(cv_hbm, vbuf, 2, c0, c0))

    def score_body(c, carry):
        slot = c % 2

        @pl.when(c + 1 < n_chunks)
        def _():
            start(kit_copies(c + 1, 1 - slot))

        wait(kit_copies(c, slot))
        s_main[c] = head_sum(jnp.dot(qi, kitbuf[slot].astype(BF16), preferred_element_type=F32))
        return carry

    lax.fori_loop(0, n_chunks, score_body, 0)
    col = lax.broadcasted_iota(jnp.int32, (n_tok, LANES), 1)
    tok = lax.broadcasted_iota(jnp.int32, (n_tok, LANES), 0)
    s_new = head_sum(lax.dot_general(qi, kin_ref[0], _NT, preferred_element_type=F32))
    s_tail[0] = jnp.where(col <= tok, s_new, -jnp.inf)

    thr = _select_threshold([(s_main, n_chunks), (s_tail, 1)], (n_tok, 1), top_k, key_axis=1)

    rows = GROUP * n_tok
    thr_rows = jnp.concatenate([thr] * GROUP, axis=0)

    def attend(state, kv_of, s_tok):
        sel = jnp.concatenate([s_tok] * GROUP, axis=0) >= thr_rows
        new_state = []
        for g in range(N_KV_HEADS):
            kt, vt = kv_of(g)
            new_state.append(_softmax_step(q_ref[0, g], kt, vt, sel, *state[g]))
        return tuple(new_state)

    def attend_body(c, state):
        slot = c % kv_slots
        ahead = c + kv_slots - 1

        @pl.when(ahead < n_chunks)
        def _():
            start(kv_copies(ck_hbm, kbuf, 1, ahead, ahead % kv_slots))
            start(kv_copies(cv_hbm, vbuf, 2, ahead, ahead % kv_slots))

        wait(kv_copies(ck_hbm, kbuf, 1, c, slot))
        wait(kv_copies(cv_hbm, vbuf, 2, c, slot))

        def kv_of(g):
            head_rows = pl.ds(g, chunk, stride=N_KV_HEADS)
            return kbuf[slot, head_rows, :].astype(BF16), vbuf[slot, head_rows, :].astype(BF16)

        return attend(state, kv_of, s_main[c])

    init = tuple((jnp.full((rows, 1), MASKED_LOGIT, F32), jnp.zeros((rows, 1), F32),
                  jnp.zeros((rows, HEAD_DIM), F32)) for _ in range(N_KV_HEADS))
    state = lax.fori_loop(0, n_chunks, attend_body, init)

    def new_kv(g):
        cols = slice(g * HEAD_DIM, (g + 1) * HEAD_DIM)
        return kn_ref[0, :, cols], vn_ref[0, :, cols]

    state = attend(state, new_kv, s_tail[0])
    for g in range(N_KV_HEADS):
        _, l_fin, acc_fin = state[g]
        o_ref[0, g] = acc_fin * (1.0 / l_fin)


def _sample_attention(qb, qib, wq, kb, vb, kib, cache_k_rows, cache_v_rows, cache_kidx_t, page_table, layer):
    nb, n_pages = page_table.shape
    n_tok = qb.shape[0] // nb
    page = cache_kidx_t.shape[3]
    assert n_tok == SUBLANES and n_tok <= LANES
    pages_per_chunk = 16
    while n_pages % pages_per_chunk:
        pages_per_chunk //= 2
    n_chunks = n_pages // pages_per_chunk
    chunk = pages_per_chunk * page
    top_k = min(TOPK_MAX, (n_pages * page + n_tok) // 4)

    q_s = qb.reshape(nb, n_tok, N_KV_HEADS, GROUP, HEAD_DIM).transpose(0, 2, 3, 1, 4)
    q_s = q_s.reshape(nb, N_KV_HEADS, GROUP * n_tok, HEAD_DIM)
    qi_s = qib.reshape(nb, n_tok, N_IDX_HEADS, IDX_DIM).transpose(0, 2, 1, 3)
    qi_s = qi_s.reshape(nb, N_IDX_HEADS * n_tok, IDX_DIM)
    wq_s = wq.reshape(nb, n_tok, N_IDX_HEADS)
    pad = lambda a: jnp.pad(a.reshape(nb, n_tok, a.shape[-1]), ((0, 0), (0, LANES - n_tok), (0, 0)))
    kin_s, kn_s, vn_s = pad(kib), pad(kb), pad(vb)

    seq3 = lambda b, pt: (b, 0, 0)
    seq4 = lambda b, pt: (b, 0, 0, 0)
    out = pl.pallas_call(
        functools.partial(_sample_attn_kernel, layer=layer, top_k=top_k,
                          pages_per_chunk=pages_per_chunk),
        grid_spec=pltpu.PrefetchScalarGridSpec(
            num_scalar_prefetch=1,
            grid=(nb,),
            in_specs=[
                pl.BlockSpec((1, N_IDX_HEADS * n_tok, IDX_DIM), seq3),
                pl.BlockSpec((1, n_tok, N_IDX_HEADS), seq3),
                pl.BlockSpec((1, N_KV_HEADS, GROUP * n_tok, HEAD_DIM), seq4),
                pl.BlockSpec((1, LANES, IDX_DIM), seq3),
                pl.BlockSpec((1, LANES, KV_DIM), seq3),
                pl.BlockSpec((1, LANES, KV_DIM), seq3),
                pl.BlockSpec(memory_space=pl.ANY),
                pl.BlockSpec(memory_space=pl.ANY),
                pl.BlockSpec(memory_space=pl.ANY),
            ],
            out_specs=pl.BlockSpec((1, N_KV_HEADS, GROUP * n_tok, HEAD_DIM), seq4),
            scratch_shapes=[
                pltpu.VMEM((KV_SLOTS, chunk * N_KV_HEADS, HEAD_DIM), F32),
                pltpu.VMEM((KV_SLOTS, chunk * N_KV_HEADS, HEAD_DIM), F32),
                pltpu.VMEM((2, IDX_DIM, chunk), F32),
                pltpu.VMEM((n_chunks, n_tok, chunk), F32),
                pltpu.VMEM((1, n_tok, LANES), F32),
                pltpu.SemaphoreType.DMA((3, KV_SLOTS)),
            ],
        ),
        out_shape=jax.ShapeDtypeStruct((nb, N_KV_HEADS, GROUP * n_tok, HEAD_DIM), F32),
        compiler_params=pltpu.CompilerParams(
            dimension_semantics=("arbitrary",), vmem_limit_bytes=V7X_VMEM_LIMIT_BYTES),
        name="attn_sample",
    )(page_table, qi_s, wq_s, q_s, kin_s, kn_s, vn_s, cache_k_rows, cache_v_rows, cache_kidx_t)
    out = out.reshape(nb, N_KV_HEADS, GROUP, n_tok, HEAD_DIM).transpose(0, 3, 1, 2, 4)
    return out.reshape(nb * n_tok, ATT_DIM)


def _pack_w_in(w):
    ends, off = [], 0
    for size in (CONV_DIM, CONV_DIM, CONV_DIM, CONV_DIM, ATT_DIM, KV_DIM, KV_DIM, QI_DIM, IDX_DIM,
                 N_IDX_HEADS, ATT_DIM, D_MODEL, D_MODEL):
        ends.append((off, off + size))
        off += size
    (uh, ub, uc, za, q, k, v, qi, ki, wi, zb, ga, gb) = [w[:, a:b] for a, b in ends]
    pad = jnp.zeros((w.shape[0], LANES - IDX_DIM - N_IDX_HEADS), w.dtype)
    return jnp.concatenate([uh, ub, uc, za, q, k, v, qi, zb, ga, gb, ki, wi, pad], axis=1).astype(BF16)


def kernel(x_prompt, x_sample, c_prompt, c_sample, cache_k, cache_v, cache_kidx, state_conv, page_table,
           norm_g, w_mod, b_mod, w_in, conv_w, q_norm, k_norm, w_proj_a, w_proj_b, w_out):
    depth = w_in.shape[0]
    bp, seq, _ = x_prompt.shape
    nb, n_tok, _ = x_sample.shape
    assert bp == 1
    n_phys, page = cache_k.shape[1], cache_k.shape[2]
    cache_k = cache_k.reshape(depth, n_phys, page * N_KV_HEADS, HEAD_DIM)
    cache_v = cache_v.reshape(depth, n_phys, page * N_KV_HEADS, HEAD_DIM)
    cache_kidx = jnp.swapaxes(cache_kidx, 2, 3)

    mod = _modulation(jnp.concatenate([c_prompt, c_sample], axis=0), w_mod, b_mod)

    yp = x_prompt.reshape(seq, D_MODEL)
    ys = x_sample.reshape(nb * n_tok, D_MODEL)
    outs = [[] for _ in range(8)]
    for l in range(depth):
        w_in_p = _pack_w_in(w_in[l])
        w_pa, w_pb, w_o = (w_proj_a[l].astype(BF16), w_proj_b[l].astype(BF16), w_out[l].astype(BF16))
        ng, qn, kn = norm_g[l][None, :], q_norm[l][None, :], k_norm[l][None, :]

        shift, scale, gate = jnp.split(mod[l, 0:bp], 3, axis=-1)
        (ap, qb, kf, kb, vf, vb, qib, kif, kib, wq, zb, sgb, ust, kmx) = _projection(
            yp, ng, scale, shift, w_in_p, conv_w[l], qn, kn, w_pa)
        attn = _prompt_attention(qb, qib, wq, kb, vb, kib, kmx)
        yp = _output(ap, attn, zb, sgb, yp, gate, w_pb, w_o)
        outs[0].append(kf.reshape(bp, seq, N_KV_HEADS, HEAD_DIM))
        outs[1].append(vf.reshape(bp, seq, N_KV_HEADS, HEAD_DIM))
        outs[2].append(kif.reshape(bp, seq, IDX_DIM))
        outs[3].append(ust[SUBLANES - (CONV_WIDTH - 1):].reshape(bp, CONV_WIDTH - 1, CONV_DIM))

        shift, scale, gate = [jnp.repeat(m, n_tok, axis=0) for m in jnp.split(mod[l, bp:], 3, axis=-1)]
        st = state_conv[l]
        zeros = jnp.zeros((nb, n_tok - 2, CONV_DIM), F32)
        p1 = jnp.concatenate([st[:, 1:2], st[:, 1:2], zeros], axis=1).reshape(nb * n_tok, CONV_DIM)
        p2 = jnp.concatenate([st[:, 0:1], st[:, 1:2], zeros], axis=1).reshape(nb * n_tok, CONV_DIM)
        (ap, qb, kf, kb, vf, vb, qib, kif, kib, wq, zb, sgb, ust, _) = _projection(
            ys, ng, scale, shift, w_in_p, conv_w[l], qn, kn, w_pa, prev=(p1, p2), seg_len=n_tok)
        attn = _sample_attention(qb, qib, wq, kb, vb, kib, cache_k, cache_v, cache_kidx, page_table, l)
        ys = _output(ap, attn, zb, sgb, ys, gate, w_pb, w_o)
        outs[4].append(kf.reshape(nb, n_tok, N_KV_HEADS, HEAD_DIM))
        outs[5].append(vf.reshape(nb, n_tok, N_KV_HEADS, HEAD_DIM))
        outs[6].append(kif.reshape(nb, n_tok, IDX_DIM))
        outs[7].append(ust.reshape(nb, n_tok, CONV_DIM)[:, n_tok - (CONV_WIDTH - 1):])

    return (yp.reshape(bp, seq, D_MODEL), ys.reshape(nb, n_tok, D_MODEL),
            *[jnp.stack(o) for o in outs])
```

```python
import functools

import jax
import jax.numpy as jnp
from jax import lax
from jax.experimental import pallas as pl
from jax.experimental.pallas import tpu as pltpu

F32 = jnp.float32
BF16 = jnp.bfloat16

D_MODEL = 1024
CONV_DIM = 1024
CONV_WIDTH = 3
N_HEADS = 8
N_KV_HEADS = 2
HEAD_DIM = 128
ATT_DIM = N_HEADS * HEAD_DIM
KV_DIM = N_KV_HEADS * HEAD_DIM
GROUP = N_HEADS // N_KV_HEADS
N_IDX_HEADS = 8
IDX_DIM = 64
QI_DIM = N_IDX_HEADS * IDX_DIM
TOPK_MAX = 256
EPS = 1e-6
MASKED_LOGIT = -1e30
LOG2_E = 1.4426950408889634
MAX_STATIC_BOUND = 50.0
BOUND_PAD = 1.02

LANES = 128
SUBLANES = 8
V7X_VMEM_LIMIT_BYTES = 56 * 1024 * 1024

OFF_UH = 0
OFF_UB = OFF_UH + CONV_DIM
OFF_UC = OFF_UB + CONV_DIM
OFF_ZA = OFF_UC + CONV_DIM
OFF_Q = OFF_ZA + CONV_DIM
OFF_K = OFF_Q + ATT_DIM
OFF_V = OFF_K + KV_DIM
OFF_QI = OFF_V + KV_DIM
OFF_ZB = OFF_QI + QI_DIM
OFF_GA = OFF_ZB + ATT_DIM
OFF_GB = OFF_GA + D_MODEL
OFF_KW = OFF_GB + D_MODEL
P_COLS = OFF_KW + LANES

KEY_LOWEST_FINITE = -2139095040
INT_MIN = -(2 ** 31)

_NT = (((1,), (1,)), ((), ()))


def _silu(x):
    return x * jax.nn.sigmoid(x)


def _rms_rows(x, g):
    ms = jnp.mean(x * x, axis=-1, keepdims=True)
    return (x * lax.rsqrt(ms + EPS)) * g


def _mod_kernel(c_ref, w_ref, b_ref, o_ref):
    s = _silu(c_ref[...]).astype(BF16)
    o_ref[0] = jnp.dot(s, w_ref[0].astype(BF16), preferred_element_type=F32) + b_ref[0]


def _modulation(c_all, w_mod, b_mod):
    depth = w_mod.shape[0]
    nb = c_all.shape[0]
    return pl.pallas_call(
        _mod_kernel,
        grid=(depth,),
        in_specs=[
            pl.BlockSpec((nb, D_MODEL), lambda l: (0, 0)),
            pl.BlockSpec((1, D_MODEL, 3 * D_MODEL), lambda l: (l, 0, 0)),
            pl.BlockSpec((1, 1, 3 * D_MODEL), lambda l: (l, 0, 0)),
        ],
        out_specs=pl.BlockSpec((1, nb, 3 * D_MODEL), lambda l: (l, 0, 0)),
        out_shape=jax.ShapeDtypeStruct((depth, nb, 3 * D_MODEL), F32),
        compiler_params=pltpu.CompilerParams(
            dimension_semantics=("arbitrary",), vmem_limit_bytes=V7X_VMEM_LIMIT_BYTES),
        name="adaln_mod",
    )(c_all, w_mod, b_mod.reshape(depth, 1, 3 * D_MODEL))


def _proj_kernel(*refs, seg_len):
    if seg_len is None:
        (x_ref, ng_ref, sc_ref, sh_ref, w_ref, cw_ref, qn_ref, kn_ref, wpa_ref,
         ap_ref, qb_ref, kf_ref, kb_ref, vf_ref, vb_ref, qib_ref, kif_ref, kib_ref, wq_ref,
         zb_ref, sgb_ref, ust_ref, kmx_ref, carry_ref) = refs
    else:
        (x_ref, ng_ref, sc_ref, sh_ref, w_ref, cw_ref, qn_ref, kn_ref, wpa_ref, p1_ref, p2_ref,
         ap_ref, qb_ref, kf_ref, kb_ref, vf_ref, vb_ref, qib_ref, kif_ref, kib_ref, wq_ref,
         zb_ref, sgb_ref, ust_ref, kmx_ref) = refs
    tm = x_ref.shape[0]

    h = _rms_rows(x_ref[...], ng_ref[...]) * (1.0 + sc_ref[...]) + sh_ref[...]
    hb = h.astype(BF16)

    def proj(off, width):
        return jnp.dot(hb, w_ref[:, off:off + width], preferred_element_type=F32)

    u = proj(OFF_UC, CONV_DIM) * proj(OFF_UH, CONV_DIM)
    row = lax.broadcasted_iota(jnp.int32, (tm, 1), 0)
    if seg_len is None:
        @pl.when(pl.program_id(0) == 0)
        def _():
            carry_ref[...] = jnp.zeros_like(carry_ref)
        prev2 = carry_ref[SUBLANES - 2:SUBLANES - 1, :]
        prev1 = carry_ref[SUBLANES - 1:SUBLANES, :]
        pos = row
        p1 = prev1
        p2 = jnp.where(pos == 0, prev2, prev1)
    else:
        pos = row % seg_len
        p1 = p1_ref[...]
        p2 = p2_ref[...]
    u1 = jnp.where(pos == 0, p1, pltpu.roll(u, 1, 0))
    u2 = jnp.where(pos < 2, p2, pltpu.roll(u, 2, 0))
    conv = (cw_ref[0:1, :] * u2 + cw_ref[1:2, :] * u1) + cw_ref[2:3, :] * u
    if seg_len is None:
        carry_ref[...] = u[tm - SUBLANES:tm, :]
        ust_ref[...] = u[tm - SUBLANES:tm, :]
    else:
        ust_ref[...] = u
    branch_a = (proj(OFF_UB, CONV_DIM) * conv) * _silu(proj(OFF_ZA, CONV_DIM))
    ap_ref[...] = jax.nn.sigmoid(proj(OFF_GA, D_MODEL)) * jnp.dot(
        branch_a.astype(BF16), wpa_ref[...], preferred_element_type=F32)

    qf = proj(OFF_Q, ATT_DIM)
    for hd in range(N_HEADS):
        sl = slice(hd * HEAD_DIM, (hd + 1) * HEAD_DIM)
        qb_ref[:, sl] = (_rms_rows(qf[:, sl], qn_ref[...]) * (HEAD_DIM ** -0.5 * LOG2_E)).astype(BF16)
    kf = proj(OFF_K, KV_DIM)

    @pl.when(pl.program_id(0) == 0)
    def _():
        kmx_ref[...] = jnp.zeros_like(kmx_ref)

    for g in range(N_KV_HEADS):
        sl = slice(g * HEAD_DIM, (g + 1) * HEAD_DIM)
        kn = _rms_rows(kf[:, sl], kn_ref[...])
        kf_ref[:, sl] = kn
        kb_ref[:, sl] = kn.astype(BF16)
        kr = kn.astype(BF16).astype(F32)
        n2 = jnp.max(jnp.sum(kr * kr, axis=-1, keepdims=True), axis=0, keepdims=True)
        kmx_ref[g:g + 1, :] = jnp.maximum(kmx_ref[g:g + 1, :], n2)
    vf = proj(OFF_V, KV_DIM)
    vf_ref[...] = vf
    vb_ref[...] = vf.astype(BF16)
    qib_ref[...] = proj(OFF_QI, QI_DIM).astype(BF16)
    kw = proj(OFF_KW, LANES)
    kif_ref[...] = kw[:, 0:IDX_DIM]
    kib_ref[...] = kw[:, 0:IDX_DIM].astype(BF16)
    wq_ref[...] = (kw[:, IDX_DIM:IDX_DIM + N_IDX_HEADS] * (N_IDX_HEADS ** -0.5)) * (IDX_DIM ** -0.5)
    zb_ref[...] = _silu(proj(OFF_ZB, ATT_DIM)).astype(BF16)
    sgb_ref[...] = jax.nn.sigmoid(proj(OFF_GB, D_MODEL))


def _row_tile(t):
    for tm in (512, 256, 128, 64, 32, 16, 8):
        if t % tm == 0:
            return tm
    raise ValueError(f"row count {t} is not a multiple of {SUBLANES}")


def _projection(x, ng, scale, shift, w_in_p, conv_w, qn, kn, w_pa, prev=None, seg_len=None):
    t = x.shape[0]
    tm = _row_tile(t) if seg_len is None else t
    gr = scale.shape[0]
    assert gr in (1, t)
    row = lambda i: (i, 0)
    const = lambda i: (0, 0)
    mod_spec = pl.BlockSpec((1 if gr == 1 else tm, D_MODEL), const if gr == 1 else row)
    in_specs = [
        pl.BlockSpec((tm, D_MODEL), row),
        pl.BlockSpec((1, D_MODEL), const),
        mod_spec, mod_spec,
        pl.BlockSpec((D_MODEL, P_COLS), const),
        pl.BlockSpec((CONV_WIDTH, CONV_DIM), const),
        pl.BlockSpec((1, HEAD_DIM), const),
        pl.BlockSpec((1, HEAD_DIM), const),
        pl.BlockSpec((CONV_DIM, D_MODEL), const),
    ]
    args = [x, ng, scale, shift, w_in_p, conv_w, qn, kn, w_pa]
    scratch = []
    if seg_len is None:
        ust_shape, ust_spec = (SUBLANES, CONV_DIM), pl.BlockSpec((SUBLANES, CONV_DIM), const)
        scratch.append(pltpu.VMEM((SUBLANES, CONV_DIM), F32))
    else:
        in_specs += [pl.BlockSpec((tm, CONV_DIM), row)] * 2
        args += list(prev)
        ust_shape, ust_spec = (t, CONV_DIM), pl.BlockSpec((tm, CONV_DIM), row)
    widths = [(D_MODEL, F32), (ATT_DIM, BF16), (KV_DIM, F32), (KV_DIM, BF16), (KV_DIM, F32),
              (KV_DIM, BF16), (QI_DIM, BF16), (IDX_DIM, F32), (IDX_DIM, BF16), (N_IDX_HEADS, F32),
              (ATT_DIM, BF16), (D_MODEL, F32)]
    out_shape = [jax.ShapeDtypeStruct((t, w), dt) for w, dt in widths]
    out_specs = [pl.BlockSpec((tm, w), row) for w, _ in widths]
    out_shape.append(jax.ShapeDtypeStruct(ust_shape, F32))
    out_specs.append(ust_spec)
    out_shape.append(jax.ShapeDtypeStruct((SUBLANES, LANES), F32))
    out_specs.append(pl.BlockSpec((SUBLANES, LANES), const))
    return pl.pallas_call(
        functools.partial(_proj_kernel, seg_len=seg_len),
        grid=(t // tm,),
        in_specs=in_specs,
        out_specs=out_specs,
        out_shape=out_shape,
        scratch_shapes=scratch,
        compiler_params=pltpu.CompilerParams(
            dimension_semantics=("arbitrary",), vmem_limit_bytes=V7X_VMEM_LIMIT_BYTES),
        name="proj_prompt" if seg_len is None else "proj_sample",
    )(*args)


def _out_kernel(ap_ref, at_ref, zb_ref, sgb_ref, x_ref, gate_ref, wpb_ref, wout_ref, y_ref):
    branch_b = (at_ref[...].astype(F32) * zb_ref[...].astype(F32)).astype(BF16)
    merged = ap_ref[...] + sgb_ref[...] * jnp.dot(branch_b, wpb_ref[...], preferred_element_type=F32)
    y_ref[...] = x_ref[...] + gate_ref[...] * jnp.dot(
        merged.astype(BF16), wout_ref[...], preferred_element_type=F32)


def _output(a_part, attn, zb, sgb, x, gate, w_pb, w_out):
    t = x.shape[0]
    tm = _row_tile(t)
    gr = gate.shape[0]
    assert gr in (1, t)
    row = lambda i: (i, 0)
    const = lambda i: (0, 0)
    act = pl.BlockSpec((tm, D_MODEL), row)
    return pl.pallas_call(
        _out_kernel,
        grid=(t // tm,),
        in_specs=[act, act, act, act, act,
                  pl.BlockSpec((1 if gr == 1 else tm, D_MODEL), const if gr == 1 else row),
                  pl.BlockSpec((ATT_DIM, D_MODEL), const),
                  pl.BlockSpec((D_MODEL, D_MODEL), const)],
        out_specs=act,
        out_shape=jax.ShapeDtypeStruct((t, D_MODEL), F32),
        compiler_params=pltpu.CompilerParams(
            dimension_semantics=("arbitrary",), vmem_limit_bytes=V7X_VMEM_LIMIT_BYTES),
        name="out_proj",
    )(a_part, attn, zb, sgb, x, gate, w_pb, w_out)


def _key_to_float(key):
    bits = jnp.where(key >= 0, key, key ^ jnp.int32(0x7FFFFFFF))
    return lax.bitcast_convert_type(bits, F32)


def _tree_sum(parts):
    while len(parts) > 1:
        parts = [parts[i] + parts[i + 1] for i in range(0, len(parts) - 1, 2)] + (
            [parts[-1]] if len(parts) % 2 else [])
    return parts[0]


def _fold_keys(x, key_axis, rows=SUBLANES):
    if key_axis == 0:
        return _tree_sum([x[c * rows:(c + 1) * rows] for c in range(x.shape[0] // rows)])
    return _tree_sum([x[:, c * LANES:(c + 1) * LANES] for c in range(x.shape[1] // LANES)])


COUNT_UNROLL = 4


def _loop_tiles(n_tiles, body, init):
    if isinstance(n_tiles, int) and n_tiles <= 16:
        for j in range(n_tiles):
            init = body(j, init)
        return init

    def group_body(n, carry):
        for t in range(COUNT_UNROLL):
            carry = body(n * COUNT_UNROLL + t, carry)
        return carry

    n_groups = n_tiles // COUNT_UNROLL
    init = lax.fori_loop(0, n_groups, group_body, init)
    return lax.fori_loop(n_groups * COUNT_UNROLL, n_tiles, body, init)


def _count(segments, pred, stat_shape, key_axis):
    fold_shape = (SUBLANES, stat_shape[1]) if key_axis == 0 else (stat_shape[0], LANES)
    total = jnp.zeros(fold_shape, F32)
    for ref, n_tiles in segments:
        def body(j, cnt, ref=ref):
            return cnt + _fold_keys(jnp.where(pred(ref[j]), 1.0, 0.0), key_axis)
        total = _loop_tiles(n_tiles, body, total)
    return jnp.sum(total, axis=key_axis, keepdims=True)


def _truncate_to_bf16(x):
    bits = lax.bitcast_convert_type(x, jnp.int32) & jnp.int32(-65536)
    return lax.bitcast_convert_type(bits, F32).astype(BF16)


def _count_coarse(coarse, thr):
    pack = 2 * SUBLANES
    total = jnp.zeros((pack, thr.shape[1]), F32)
    one, zero = jnp.ones((), thr.dtype), jnp.zeros((), thr.dtype)
    for ref, n_tiles in coarse:
        def body(j, cnt, ref=ref):
            hit = jnp.where(ref[j] >= thr, one, zero)
            return cnt + _fold_keys(hit, 0, rows=pack).astype(F32)
        total = _loop_tiles(n_tiles, body, total)
    return jnp.sum(total, axis=0, keepdims=True)


def _merge_two_smallest(a, b):
    return jnp.minimum(a[0], b[0]), jnp.minimum(jnp.maximum(a[0], b[0]), jnp.minimum(a[1], b[1]))


def _two_smallest(segments, thr, key_axis):
    fold_shape = (SUBLANES, thr.shape[1]) if key_axis == 0 else (thr.shape[0], LANES)
    best = (jnp.full(fold_shape, jnp.inf, F32), jnp.full(fold_shape, jnp.inf, F32))
    for ref, n_tiles in segments:
        def body(j, carry, ref=ref):
            tile = ref[j]
            x = jnp.where(tile >= thr, tile, jnp.inf)
            if key_axis == 0:
                parts = [x[c * SUBLANES:(c + 1) * SUBLANES] for c in range(x.shape[0] // SUBLANES)]
            else:
                parts = [x[:, c * LANES:(c + 1) * LANES] for c in range(x.shape[1] // LANES)]
            if len(parts) % 2:
                parts.append(jnp.full(fold_shape, jnp.inf, F32))
            pairs = [(jnp.minimum(parts[c], parts[c + 1]), jnp.maximum(parts[c], parts[c + 1]))
                     for c in range(0, len(parts), 2)]
            while len(pairs) > 1:
                pairs = [_merge_two_smallest(pairs[c], pairs[c + 1]) for c in range(0, len(pairs) - 1, 2)] + (
                    [pairs[-1]] if len(pairs) % 2 else [])
            return _merge_two_smallest(carry, pairs[0])
        best = _loop_tiles(n_tiles, body, best)
    first, second = best
    low1 = jnp.min(first, axis=key_axis, keepdims=True)
    at_low1 = first == low1
    n_low1 = jnp.sum(jnp.where(at_low1, 1.0, 0.0), axis=key_axis, keepdims=True)
    others = jnp.min(jnp.where(at_low1, second, first), axis=key_axis, keepdims=True)
    return low1, jnp.where(n_low1 >= 2.0, low1, others)


def _kth_largest(segments, stat_shape, k, key_axis, coarse=None):
    kf = float(k)

    def step(i, state, count_fn):
        prefix, cge = state
        cand = prefix ^ jnp.left_shift(jnp.int32(1), 31 - i)
        cnt = count_fn(_key_to_float(cand))
        take = cnt >= kf
        return jnp.where(take, cand, prefix), jnp.where(take, cnt, cge)

    def count_fine(t):
        return _count(segments, lambda tile: tile >= t, stat_shape, key_axis)

    state = (jnp.full(stat_shape, INT_MIN, jnp.int32), jnp.full(stat_shape, jnp.inf, F32))
    first_fine = 0
    if coarse is not None:
        assert key_axis == 0
        first_fine = 16
        state = lax.fori_loop(
            0, first_fine,
            lambda i, st: step(i, st, lambda t: _count_coarse(coarse, _truncate_to_bf16(t))), state)

    def cond(carry):
        i, _, cge = carry
        return (i < 32) & (jnp.max(cge) > kf + 1.0)

    def body(carry):
        i, prefix, cge = carry
        return (i + 2,) + step(i + 1, step(i, (prefix, cge), count_fine), count_fine)

    _, prefix, cge = lax.while_loop(cond, body, (jnp.int32(first_fine),) + state)
    thr = _key_to_float(jnp.maximum(prefix, jnp.int32(KEY_LOWEST_FINITE)))
    cge = jnp.where(cge == jnp.inf, 0.0, cge)

    low1, low2 = _two_smallest(segments, thr, key_axis)
    one_over = cge == kf + 1.0
    distinct = one_over & (low2 > low1)
    thr = jnp.where(one_over, jnp.where(distinct, low2, low1), thr)
    return thr, jnp.where(distinct, kf, cge)


def _drop_excess_ties(segments, thr, cge, k, key_axis):
    excess = cge > float(k)
    n_eq = _count(segments, lambda tile: tile == thr, thr.shape, key_axis)
    need = float(k) - (cge - n_eq)
    seen = jnp.zeros(thr.shape, F32)
    for ref, n_tiles in segments:
        width = ref.shape[1 + key_axis]
        cw = min(width, 2 * LANES)
        r_i = lax.broadcasted_iota(jnp.int32, (cw, cw), 0)
        c_i = lax.broadcasted_iota(jnp.int32, (cw, cw), 1)
        tri = jnp.where(r_i >= c_i if key_axis == 0 else r_i <= c_i, 1.0, 0.0).astype(BF16)

        def drop_body(j, seen, ref=ref, width=width, cw=cw, tri=tri):
            for c in range(width // cw):
                idx = (j, slice(c * cw, (c + 1) * cw), slice(None)) if key_axis == 0 else (
                    j, slice(None), slice(c * cw, (c + 1) * cw))
                tile = ref[idx]
                eq = tile == thr
                eq_b = jnp.where(eq, 1.0, 0.0).astype(BF16)
                within = (jnp.dot(tri, eq_b, preferred_element_type=F32) if key_axis == 0
                          else jnp.dot(eq_b, tri, preferred_element_type=F32))
                drop = eq & excess & (seen + within > need)
                ref[idx] = jnp.where(drop, -jnp.inf, tile)
                seen = seen + jnp.sum(jnp.where(eq, 1.0, 0.0), axis=key_axis, keepdims=True)
            return seen

        seen = lax.fori_loop(0, n_tiles, drop_body, seen)


def _select_threshold(segments, stat_shape, k, key_axis, coarse=None):
    thr, cge = _kth_largest(segments, stat_shape, k, key_axis, coarse)

    @pl.when(jnp.max(cge) > float(k))
    def _():
        _drop_excess_ties(segments, thr, cge, k, key_axis)

    return thr


def _softmax_step_t(q, kt, vt_aug, sels, m_old, acc_old):
    width = sels[0].shape[1]
    rows = sels[0].shape[0]
    s = lax.dot_general(kt, q, _NT, preferred_element_type=F32)
    s = jnp.concatenate(
        [jnp.concatenate([jnp.where(sel, s[t * rows:(t + 1) * rows, r * width:(r + 1) * width], MASKED_LOGIT)
                          for r in range(q.shape[0] // width)], axis=1)
         for t, sel in enumerate(sels)], axis=0)
    m_new = jnp.maximum(m_old, jnp.max(s, axis=0, keepdims=True))
    p = jnp.exp2(s - m_new)
    alpha = jnp.exp2(m_old - m_new)
    acc_new = alpha * acc_old + jnp.dot(vt_aug, p.astype(BF16), preferred_element_type=F32)
    return m_new, acc_new


def _prompt_attn_kernel(q_ref, qi_ref, wqt_ref, k_ref, vt_ref, ki_ref, kmx_ref, o_ref,
                        s_ref, c_ref, m_ref, acc_ref, *, top_k):
    tq = q_ref.shape[0]
    tk = s_ref.shape[1]
    i = pl.program_id(0)

    qi = qi_ref[...]
    wqt = wqt_ref[...]
    qi_all = jnp.concatenate([qi[:, h * IDX_DIM:(h + 1) * IDX_DIM] for h in range(N_IDX_HEADS)], axis=0)
    w_all = jnp.concatenate([wqt[h:h + 1, :] for h in range(N_IDX_HEADS)], axis=1)

    def scores(kt):
        terms = jnp.maximum(lax.dot_general(kt, qi_all, _NT, preferred_element_type=F32), 0.0) * w_all
        acc = terms[:, 0:tq]
        for h in range(1, N_IDX_HEADS):
            acc = acc + terms[:, h * tq:(h + 1) * tq]
        return acc + 0.0

    def key_rows(j):
        return pl.ds(pl.multiple_of(j * tk, tk), tk)

    def put_scores(j, sc):
        s_ref[j] = sc
        c_ref[j] = _truncate_to_bf16(sc)

    span = vt_ref.shape[3] // tk

    def score_span_body(n, carry):
        rows = pl.ds(pl.multiple_of(n * (span * tk), span * tk), span * tk)
        sc = scores(ki_ref[rows, :])
        for t in range(span):
            put_scores(n * span + t, sc[t * tk:(t + 1) * tk])
        return carry

    def score_body(j, carry):
        put_scores(j, scores(ki_ref[key_rows(j), :]))
        return carry

    lax.fori_loop(0, i // span, score_span_body, 0)
    lax.fori_loop((i // span) * span, i, score_body, 0)
    visible = (lax.broadcasted_iota(jnp.int32, (tk, tq), 0)
               <= lax.broadcasted_iota(jnp.int32, (tk, tq), 1))
    put_scores(i, jnp.where(visible, scores(ki_ref[key_rows(i), :]), -jnp.inf))

    thr = _select_threshold([(s_ref, i + 1)], (1, tq), top_k, key_axis=0, coarse=[(c_ref, i + 1)])

    n_steps = (i + span) // span
    for extra in range(1, span):
        @pl.when(i + extra < n_steps * span)
        def _():
            s_ref[i + extra] = jnp.full((tk, tq), -jnp.inf, F32)
    acc_ref[...] = jnp.zeros(acc_ref.shape, F32)

    q_groups = [jnp.concatenate([q_ref[:, h * HEAD_DIM:(h + 1) * HEAD_DIM]
                                 for h in range(g * GROUP, (g + 1) * GROUP)], axis=0)
                for g in range(N_KV_HEADS)]

    ones = jnp.ones((SUBLANES, HEAD_DIM), BF16)

    def sq_norms(h):
        qf = q_ref[:, h * HEAD_DIM:(h + 1) * HEAD_DIM].astype(F32)
        return lax.dot_general(ones, (qf * qf).astype(BF16), _NT, preferred_element_type=F32)[0:1]

    bounds = [jnp.sqrt(jnp.concatenate([sq_norms(h) for h in range(g * GROUP, (g + 1) * GROUP)], axis=1)
                       * kmx_ref[g:g + 1, 0:1]) * BOUND_PAD
              for g in range(N_KV_HEADS)]
    bounded = jnp.max(jnp.maximum(bounds[0], bounds[1])) <= MAX_STATIC_BOUND

    def key_span(n):
        return pl.ds(pl.multiple_of(n * (span * tk), span * tk), span * tk)

    def bounded_body(n, carry):
        masks = [jnp.where(s_ref[n * span + t] >= thr, 1.0, 0.0).astype(BF16) for t in range(span)]
        mask = jnp.concatenate([jnp.concatenate([mk] * GROUP, axis=1) for mk in masks], axis=0)
        for g in range(N_KV_HEADS):
            s = lax.dot_general(k_ref[key_span(n), g * HEAD_DIM:(g + 1) * HEAD_DIM], q_groups[g], _NT,
                                preferred_element_type=F32)
            p = jnp.exp2(s - bounds[g]).astype(BF16) * mask
            acc_ref[g] = acc_ref[g] + jnp.dot(vt_ref[n, g], p, preferred_element_type=F32)
        return carry

    def running_max_body(n, carry):
        sels = [s_ref[n * span + t] >= thr for t in range(span)]
        for g in range(N_KV_HEADS):
            m_new, acc_new = _softmax_step_t(
                q_groups[g], k_ref[key_span(n), g * HEAD_DIM:(g + 1) * HEAD_DIM], vt_ref[n, g], sels,
                m_ref[g], acc_ref[g])
            m_ref[g] = m_new
            acc_ref[g] = acc_new
        return carry

    @pl.when(bounded)
    def _():
        lax.fori_loop(0, n_steps, bounded_body, 0)

    @pl.when(jnp.logical_not(bounded))
    def _():
        m_ref[...] = jnp.full(m_ref.shape, MASKED_LOGIT, F32)
        lax.fori_loop(0, n_steps, running_max_body, 0)

    for g in range(N_KV_HEADS):
        acc = acc_ref[g]
        out_t = acc[0:HEAD_DIM] * (1.0 / acc[HEAD_DIM:HEAD_DIM + 1])
        for r in range(GROUP):
            h = g * GROUP + r
            o_ref[:, h * HEAD_DIM:(h + 1) * HEAD_DIM] = out_t[:, r * tq:(r + 1) * tq].T.astype(BF16)


def _prompt_attention(qb, qib, wq, kb, vb, kib, kmx):
    t = qb.shape[0]
    tq = min(2 * LANES, t)
    assert t % tq == 0
    nt = t // tq
    span = 4 if nt % 4 == 0 else (2 if nt % 2 == 0 else 1)
    top_k = min(TOPK_MAX, t // 4)
    vt = vb.reshape(nt // span, span * tq, N_KV_HEADS, HEAD_DIM).transpose(0, 2, 3, 1)
    vt = jnp.concatenate([vt, jnp.ones((nt // span, N_KV_HEADS, SUBLANES, span * tq), BF16)], axis=2)
    row = lambda i: (i, 0)
    const = lambda i: (0, 0)
    return pl.pallas_call(
        functools.partial(_prompt_attn_kernel, top_k=top_k),
        grid=(nt,),
        in_specs=[
            pl.BlockSpec((tq, ATT_DIM), row),
            pl.BlockSpec((tq, QI_DIM), row),
            pl.BlockSpec((N_IDX_HEADS, tq), lambda i: (0, i)),
            pl.BlockSpec((t, KV_DIM), const),
            pl.BlockSpec(vt.shape, lambda i: (0, 0, 0, 0)),
            pl.BlockSpec((t, IDX_DIM), const),
            pl.BlockSpec((SUBLANES, LANES), const),
        ],
        out_specs=pl.BlockSpec((tq, ATT_DIM), row),
        out_shape=jax.ShapeDtypeStruct((t, ATT_DIM), BF16),
        scratch_shapes=[
            pltpu.VMEM((nt, tq, tq), F32),
            pltpu.VMEM((nt, tq, tq), BF16),
            pltpu.VMEM((N_KV_HEADS, 1, GROUP * tq), F32),
            pltpu.VMEM((N_KV_HEADS, HEAD_DIM + SUBLANES, GROUP * tq), F32),
        ],
        compiler_params=pltpu.CompilerParams(
            dimension_semantics=("arbitrary",), vmem_limit_bytes=V7X_VMEM_LIMIT_BYTES),
        name="attn_prompt",
    )(qb, qib, wq.T, kb, vt, kib, kmx)


def _softmax_step(q, kt, vt, sel, m_old, l_old, acc_old):
    s = lax.dot_general(q, kt, _NT, preferred_element_type=F32)
    s = jnp.where(sel, s, MASKED_LOGIT)
    m_new = jnp.maximum(m_old, jnp.max(s, axis=-1, keepdims=True))
    p = jnp.exp2(s - m_new)
    alpha = jnp.exp2(m_old - m_new)
    l_new = alpha * l_old + jnp.sum(p, axis=-1, keepdims=True)
    acc_new = alpha * acc_old + jnp.dot(p.astype(BF16), vt, preferred_element_type=F32)
    return m_new, l_new, acc_new


KV_SLOTS = 4


def _sample_attn_kernel(pt_ref, qi_ref, wq_ref, q_ref, kin_ref, kn_ref, vn_ref,
                        ck_hbm, cv_hbm, ckit_hbm, o_ref,
                        kbuf, vbuf, kitbuf, s_main, s_tail, sem, *, layer, top_k, pages_per_chunk):
    b = pl.program_id(0)
    n_chunks, n_tok, chunk = s_main.shape
    page = chunk // pages_per_chunk

    def page_id(c, p):
        return pt_ref[b, c * pages_per_chunk + p]

    def kv_copies(hbm, buf, sem_row, c, slot):
        rows = page * N_KV_HEADS
        return [pltpu.make_async_copy(hbm.at[layer, page_id(c, p)],
                                      buf.at[slot, pl.ds(p * rows, rows)], sem.at[sem_row, slot])
                for p in range(pages_per_chunk)]

    def kit_copies(c, slot):
        return [pltpu.make_async_copy(ckit_hbm.at[layer, page_id(c, p)],
                                      kitbuf.at[slot, :, pl.ds(p * page, page)], sem.at[0, slot])
                for p in range(pages_per_chunk)]

    def start(copies):
        for cp in copies:
            cp.start()

    def wait(copies):
        for cp in copies:
            cp.wait()

    qi = qi_ref[0]
    wq = wq_ref[0]
    w_cols = [wq[:, h:h + 1] for h in range(N_IDX_HEADS)]

    def head_sum(s_all):
        acc = None
        for h in range(N_IDX_HEADS):
            term = jnp.maximum(s_all[h * n_tok:(h + 1) * n_tok, :], 0.0) * w_cols[h]
            acc = term if acc is None else acc + term
        return acc + 0.0

    start(kit_copies(0, 0))
    kv_slots = kbuf.shape[0]
    for c0 in range(min(kv_slots - 1, n_chunks)):
        start(kv_copies(ck_hbm, kbuf, 1, c0, c0))
        start(kv_copies(cv_hbm, vbuf, 2, c0, c0))

    def score_body(c, carry):
        slot = c % 2

        @pl.when(c + 1 < n_chunks)
        def _():
            start(kit_copies(c + 1, 1 - slot))

        wait(kit_copies(c, slot))
        s_main[c] = head_sum(jnp.dot(qi, kitbuf[slot].astype(BF16), preferred_element_type=F32))
        return carry

    lax.fori_loop(0, n_chunks, score_body, 0)
    col = lax.broadcasted_iota(jnp.int32, (n_tok, LANES), 1)
    tok = lax.broadcasted_iota(jnp.int32, (n_tok, LANES), 0)
    s_new = head_sum(lax.dot_general(qi, kin_ref[0], _NT, preferred_element_type=F32))
    s_tail[0] = jnp.where(col <= tok, s_new, -jnp.inf)

    thr = _select_threshold([(s_main, n_chunks), (s_tail, 1)], (n_tok, 1), top_k, key_axis=1)

    rows = GROUP * n_tok
    thr_rows = jnp.concatenate([thr] * GROUP, axis=0)

    def attend(state, kv_of, s_tok):
        sel = jnp.concatenate([s_tok] * GROUP, axis=0) >= thr_rows
        new_state = []
        for g in range(N_KV_HEADS):
            kt, vt = kv_of(g)
            new_state.append(_softmax_step(q_ref[0, g], kt, vt, sel, *state[g]))
        return tuple(new_state)

    def attend_body(c, state):
        slot = c % kv_slots
        ahead = c + kv_slots - 1

        @pl.when(ahead < n_chunks)
        def _():
            start(kv_copies(ck_hbm, kbuf, 1, ahead, ahead % kv_slots))
            start(kv_copies(cv_hbm, vbuf, 2, ahead, ahead % kv_slots))

        wait(kv_copies(ck_hbm, kbuf, 1, c, slot))
        wait(kv_copies(cv_hbm, vbuf, 2, c, slot))

        def kv_of(g):
            head_rows = pl.ds(g, chunk, stride=N_KV_HEADS)
            return kbuf[slot, head_rows, :].astype(BF16), vbuf[slot, head_rows, :].astype(BF16)

        return attend(state, kv_of, s_main[c])

    init = tuple((jnp.full((rows, 1), MASKED_LOGIT, F32), jnp.zeros((rows, 1), F32),
                  jnp.zeros((rows, HEAD_DIM), F32)) for _ in range(N_KV_HEADS))
    state = lax.fori_loop(0, n_chunks, attend_body, init)

    def new_kv(g):
        cols = slice(g * HEAD_DIM, (g + 1) * HEAD_DIM)
        return kn_ref[0, :, cols], vn_ref[0, :, cols]

    state = attend(state, new_kv, s_tail[0])
    for g in range(N_KV_HEADS):
        _, l_fin, acc_fin = state[g]
        o_ref[0, g] = acc_fin * (1.0 / l_fin)


def _sample_attention(qb, qib, wq, kb, vb, kib, cache_k_rows, cache_v_rows, cache_kidx_t, page_table, layer):
    nb, n_pages = page_table.shape
    n_tok = qb.shape[0] // nb
    page = cache_kidx_t.shape[3]
    assert n_tok == SUBLANES and n_tok <= LANES
    pages_per_chunk = 16
    while n_pages % pages_per_chunk:
        pages_per_chunk //= 2
    n_chunks = n_pages // pages_per_chunk
    chunk = pages_per_chunk * page
    top_k = min(TOPK_MAX, (n_pages * page + n_tok) // 4)

    q_s = qb.reshape(nb, n_tok, N_KV_HEADS, GROUP, HEAD_DIM).transpose(0, 2, 3, 1, 4)
    q_s = q_s.reshape(nb, N_KV_HEADS, GROUP * n_tok, HEAD_DIM)
    qi_s = qib.reshape(nb, n_tok, N_IDX_HEADS, IDX_DIM).transpose(0, 2, 1, 3)
    qi_s = qi_s.reshape(nb, N_IDX_HEADS * n_tok, IDX_DIM)
    wq_s = wq.reshape(nb, n_tok, N_IDX_HEADS)
    pad = lambda a: jnp.pad(a.reshape(nb, n_tok, a.shape[-1]), ((0, 0), (0, LANES - n_tok), (0, 0)))
    kin_s, kn_s, vn_s = pad(kib), pad(kb), pad(vb)

    seq3 = lambda b, pt: (b, 0, 0)
    seq4 = lambda b, pt: (b, 0, 0, 0)
    out = pl.pallas_call(
        functools.partial(_sample_attn_kernel, layer=layer, top_k=top_k,
                          pages_per_chunk=pages_per_chunk),
        grid_spec=pltpu.PrefetchScalarGridSpec(
            num_scalar_prefetch=1,
            grid=(nb,),
            in_specs=[
                pl.BlockSpec((1, N_IDX_HEADS * n_tok, IDX_DIM), seq3),
                pl.BlockSpec((1, n_tok, N_IDX_HEADS), seq3),
                pl.BlockSpec((1, N_KV_HEADS, GROUP * n_tok, HEAD_DIM), seq4),
                pl.BlockSpec((1, LANES, IDX_DIM), seq3),
                pl.BlockSpec((1, LANES, KV_DIM), seq3),
                pl.BlockSpec((1, LANES, KV_DIM), seq3),
                pl.BlockSpec(memory_space=pl.ANY),
                pl.BlockSpec(memory_space=pl.ANY),
                pl.BlockSpec(memory_space=pl.ANY),
            ],
            out_specs=pl.BlockSpec((1, N_KV_HEADS, GROUP * n_tok, HEAD_DIM), seq4),
            scratch_shapes=[
                pltpu.VMEM((KV_SLOTS, chunk * N_KV_HEADS, HEAD_DIM), F32),
                pltpu.VMEM((KV_SLOTS, chunk * N_KV_HEADS, HEAD_DIM), F32),
                pltpu.VMEM((2, IDX_DIM, chunk), F32),
                pltpu.VMEM((n_chunks, n_tok, chunk), F32),
                pltpu.VMEM((1, n_tok, LANES), F32),
                pltpu.SemaphoreType.DMA((3, KV_SLOTS)),
            ],
        ),
        out_shape=jax.ShapeDtypeStruct((nb, N_KV_HEADS, GROUP * n_tok, HEAD_DIM), F32),
        compiler_params=pltpu.CompilerParams(
            dimension_semantics=("arbitrary",), vmem_limit_bytes=V7X_VMEM_LIMIT_BYTES),
        name="attn_sample",
    )(page_table, qi_s, wq_s, q_s, kin_s, kn_s, vn_s, cache_k_rows, cache_v_rows, cache_kidx_t)
    out = out.reshape(nb, N_KV_HEADS, GROUP, n_tok, HEAD_DIM).transpose(0, 3, 1, 2, 4)
    return out.reshape(nb * n_tok, ATT_DIM)


def _pack_w_in(w):
    ends, off = [], 0
    for size in (CONV_DIM, CONV_DIM, CONV_DIM, CONV_DIM, ATT_DIM, KV_DIM, KV_DIM, QI_DIM, IDX_DIM,
                 N_IDX_HEADS, ATT_DIM, D_MODEL, D_MODEL):
        ends.append((off, off + size))
        off += size
    (uh, ub, uc, za, q, k, v, qi, ki, wi, zb, ga, gb) = [w[:, a:b] for a, b in ends]
    pad = jnp.zeros((w.shape[0], LANES - IDX_DIM - N_IDX_HEADS), w.dtype)
    return jnp.concatenate([uh, ub, uc, za, q, k, v, qi, zb, ga, gb, ki, wi, pad], axis=1).astype(BF16)


def kernel(x_prompt, x_sample, c_prompt, c_sample, cache_k, cache_v, cache_kidx, state_conv, page_table,
           norm_g, w_mod, b_mod, w_in, conv_w, q_norm, k_norm, w_proj_a, w_proj_b, w_out):
    depth = w_in.shape[0]
    bp, seq, _ = x_prompt.shape
    nb, n_tok, _ = x_sample.shape
    assert bp == 1
    n_phys, page = cache_k.shape[1], cache_k.shape[2]
    cache_k = cache_k.reshape(depth, n_phys, page * N_KV_HEADS, HEAD_DIM)
    cache_v = cache_v.reshape(depth, n_phys, page * N_KV_HEADS, HEAD_DIM)
    cache_kidx = jnp.swapaxes(cache_kidx, 2, 3)

    mod = _modulation(jnp.concatenate([c_prompt, c_sample], axis=0), w_mod, b_mod)

    yp = x_prompt.reshape(seq, D_MODEL)
    ys = x_sample.reshape(nb * n_tok, D_MODEL)
    outs = [[] for _ in range(8)]
    for l in range(depth):
        w_in_p = _pack_w_in(w_in[l])
        w_pa, w_pb, w_o = (w_proj_a[l].astype(BF16), w_proj_b[l].astype(BF16), w_out[l].astype(BF16))
        ng, qn, kn = norm_g[l][None, :], q_norm[l][None, :], k_norm[l][None, :]

        shift, scale, gate = jnp.split(mod[l, 0:bp], 3, axis=-1)
        (ap, qb, kf, kb, vf, vb, qib, kif, kib, wq, zb, sgb, ust, kmx) = _projection(
            yp, ng, scale, shift, w_in_p, conv_w[l], qn, kn, w_pa)
        attn = _prompt_attention(qb, qib, wq, kb, vb, kib, kmx)
        yp = _output(ap, attn, zb, sgb, yp, gate, w_pb, w_o)
        outs[0].append(kf.reshape(bp, seq, N_KV_HEADS, HEAD_DIM))
        outs[1].append(vf.reshape(bp, seq, N_KV_HEADS, HEAD_DIM))
        outs[2].append(kif.reshape(bp, seq, IDX_DIM))
        outs[3].append(ust[SUBLANES - (CONV_WIDTH - 1):].reshape(bp, CONV_WIDTH - 1, CONV_DIM))

        shift, scale, gate = [jnp.repeat(m, n_tok, axis=0) for m in jnp.split(mod[l, bp:], 3, axis=-1)]
        st = state_conv[l]
        zeros = jnp.zeros((nb, n_tok - 2, CONV_DIM), F32)
        p1 = jnp.concatenate([st[:, 1:2], st[:, 1:2], zeros], axis=1).reshape(nb * n_tok, CONV_DIM)
        p2 = jnp.concatenate([st[:, 0:1], st[:, 1:2], zeros], axis=1).reshape(nb * n_tok, CONV_DIM)
        (ap, qb, kf, kb, vf, vb, qib, kif, kib, wq, zb, sgb, ust, _) = _projection(
            ys, ng, scale, shift, w_in_p, conv_w[l], qn, kn, w_pa, prev=(p1, p2), seg_len=n_tok)
        attn = _sample_attention(qb, qib, wq, kb, vb, kib, cache_k, cache_v, cache_kidx, page_table, l)
        ys = _output(ap, attn, zb, sgb, ys, gate, w_pb, w_o)
        outs[4].append(kf.reshape(nb, n_tok, N_KV_HEADS, HEAD_DIM))
        outs[5].append(vf.reshape(nb, n_tok, N_KV_HEADS, HEAD_DIM))
        outs[6].append(kif.reshape(nb, n_tok, IDX_DIM))
        outs[7].append(ust.reshape(nb, n_tok, CONV_DIM)[:, n_tok - (CONV_WIDTH - 1):])

    return (yp.reshape(bp, seq, D_MODEL), ys.reshape(nb, n_tok, D_MODEL),
            *[jnp.stack(o) for o in outs])
```
